```python
import math
import jax, jax.numpy as jnp
from jax import lax
import numpy as np

D_MODEL = 1024
BATCH = 16
SEQ = 4096
DEPTH = 2

HEAD_DIM = 64
SCALE = HEAD_DIM ** -0.5
ROPE_THETA = 10000.0
NORM_EPS = 1e-5
BAND = 128

A_HEADS = 8
A_KV_HEADS = 2
A_GROUP = A_HEADS // A_KV_HEADS
CMP_BLOCK = 32
CMP_STRIDE = 16
CMP_HIDDEN = 256
SEL_BLOCK = 64
SEL_TOPK = 16
NSA_WINDOW = 512
NSA_QBLOCK = 64
N_BRANCH = 3

B_HEADS = 8
B_KV_HEADS = 2
B_GROUP = B_HEADS // B_KV_HEADS
B_WINDOW = 128

C_HEADS = 16
C_PATTERNS = ((128, 1), (512, 4), (2048, 16))

A_Q = A_HEADS * HEAD_DIM
A_KV = A_KV_HEADS * HEAD_DIM
A_GATES = A_HEADS * N_BRANCH
B_Q = B_HEADS * HEAD_DIM
B_KV = B_KV_HEADS * HEAD_DIM
EVEN_IN = A_Q + 6 * A_KV + A_GATES + B_Q + 2 * B_KV
EVEN_MIX = A_Q + B_Q
ODD_IN = 3 * C_HEADS * HEAD_DIM
ODD_MIX = C_HEADS * HEAD_DIM
FFN_HIDDEN = -(-8 * D_MODEL // (3 * 256)) * 256
N_EVEN = (DEPTH + 1) // 2
N_ODD = DEPTH // 2

kernel_name = "hybrid_nsa_swasink_dilated_swiglu"


def rms_norm(x, g):
    xf = x.astype(jnp.float32)
    y = xf * lax.rsqrt(jnp.mean(xf * xf, axis=-1, keepdims=True) + NORM_EPS)
    return (y * g.astype(jnp.float32)).astype(x.dtype)


def rope_tables(seq, dtype):
    inv = 1.0 / (ROPE_THETA ** (jnp.arange(0, HEAD_DIM, 2, dtype=jnp.float32) / HEAD_DIM))
    ang = jnp.arange(seq, dtype=jnp.float32)[:, None] * inv[None, :]
    ang = jnp.concatenate([ang, ang], axis=-1)
    return jnp.cos(ang).astype(dtype), jnp.sin(ang).astype(dtype)


def apply_rope(x, cos, sin):
    x1, x2 = jnp.split(x, 2, axis=-1)
    rot = jnp.concatenate([-x2, x1], axis=-1)
    return x * cos[:, None, :] + rot * sin[:, None, :]


def banded_attention(q, k, v, reach, sink=None):
    L, dh = q.shape[-2], q.shape[-1]
    nb = -(-L // BAND)
    pad = nb * BAND - L
    if pad:
        q = jnp.pad(q, [(0, 0)] * (q.ndim - 2) + [(0, pad), (0, 0)])
        k = jnp.pad(k, [(0, 0)] * (k.ndim - 2) + [(0, pad), (0, 0)])
        v = jnp.pad(v, [(0, 0)] * (v.ndim - 2) + [(0, pad), (0, 0)])
    qb = q.reshape(q.shape[:-2] + (nb, BAND, dh))
    kb = k.reshape(k.shape[:-2] + (nb, BAND, dh))
    vb = v.reshape(v.shape[:-2] + (nb, BAND, dh))

    def with_prev(t):
        prev = jnp.pad(t, [(0, 0)] * (t.ndim - 3) + [(1, 0), (0, 0), (0, 0)])[..., :-1, :, :]
        return jnp.concatenate([prev, t], axis=-2)

    kc, vc = with_prev(kb), with_prev(vb)
    s = jnp.einsum('...gnqd,...nkd->...gnqk', qb, kc, preferred_element_type=jnp.float32) * SCALE
    qi = jnp.arange(BAND)[:, None] + BAND
    kj = jnp.arange(2 * BAND)[None, :]
    diff = qi - kj
    mask = (diff >= 0) & (diff <= reach)
    not_before_start = (jnp.arange(nb)[:, None, None] > 0) | (kj[None] >= BAND)
    mask = mask[None] & not_before_start
    s = jnp.where(mask, s, -jnp.inf)
    m = jnp.max(s, axis=-1)
    if sink is not None:
        m = jnp.maximum(m, sink)
    p = jnp.exp(s - m[..., None])
    den = jnp.sum(p, axis=-1)
    if sink is not None:
        den = den + jnp.exp(sink - m)
    o = jnp.einsum('...gnqk,...nkd->...gnqd', p, vc.astype(jnp.float32)) / den[..., None]
    o = o.reshape(q.shape[:-2] + (nb * BAND, dh))[..., :L, :]
    lse = (m + jnp.log(den)).reshape(q.shape[:-2] + (nb * BAND,))[..., :L]
    return o, lse


def compress_blocks(t, pe, w1, w2):
    bsz, hkv, seq, dh = t.shape
    chunks = t.reshape(bsz, hkv, seq // CMP_STRIDE, CMP_STRIDE, dh)
    blocks = jnp.concatenate([chunks[:, :, :-1], chunks[:, :, 1:]], axis=-2)
    nc = blocks.shape[2]
    blocks = (blocks + pe).reshape(bsz, hkv, nc, CMP_BLOCK * dh)
    return jax.nn.silu(blocks @ w1) @ w2


def nsa_attention(q_plain, q_rot, k_cmp, v_cmp, k_sel, v_sel, k_win, v_win, gates,
                  pe_k, pe_v, wk1, wk2, wv1, wv2):
    bsz, seq = q_plain.shape[:2]
    dh = HEAD_DIM
    nc = seq // CMP_STRIDE - 1
    nb = seq // SEL_BLOCK
    n_sel = min(SEL_TOPK, nb)

    def group_q(t):
        return t.reshape(bsz, seq, A_KV_HEADS, A_GROUP, dh).transpose(0, 2, 3, 1, 4)

    def heads_first(t):
        return t.transpose(0, 2, 1, 3)

    qp, qr = group_q(q_plain), group_q(q_rot)
    g = gates.reshape(bsz, seq, A_KV_HEADS, A_GROUP, N_BRANCH).transpose(0, 2, 3, 1, 4)
    kc = compress_blocks(heads_first(k_cmp), pe_k, wk1, wk2)
    vc = compress_blocks(heads_first(v_cmp), pe_v, wv1, wv2).astype(jnp.float32)
    ks_b = heads_first(k_sel).reshape(bsz, A_KV_HEADS, nb, SEL_BLOCK, dh)
    vs_b = heads_first(v_sel).reshape(bsz, A_KV_HEADS, nb, SEL_BLOCK, dh)
    pad_w = ((0, 0), (0, 0), (NSA_WINDOW, 0), (0, 0))
    kw_p = jnp.pad(heads_first(k_win), pad_w)
    vw_p = jnp.pad(heads_first(v_win), pad_w)

    cmp_end = jnp.arange(nc) * CMP_STRIDE + CMP_BLOCK - 1
    c_start = np.arange(nc) * CMP_STRIDE
    s_start = np.arange(nb) * SEL_BLOCK
    overlap = ((c_start[:, None] <= s_start[None, :] + SEL_BLOCK - 1) &
               (c_start[:, None] + CMP_BLOCK - 1 >= s_start[None, :]))
    cmp_to_sel = jnp.asarray(overlap.astype(np.float32))
    bi = jnp.arange(bsz)[:, None, None, None]
    hi = jnp.arange(A_KV_HEADS)[None, :, None, None]
    blk = jnp.arange(nb)[None, :]

    def query_block(i):
        q0 = i * NSA_QBLOCK
        tq = q0 + jnp.arange(NSA_QBLOCK)
        qp_b = lax.dynamic_slice_in_dim(qp, q0, NSA_QBLOCK, axis=3)
        qr_b = lax.dynamic_slice_in_dim(qr, q0, NSA_QBLOCK, axis=3)
        g_b = lax.dynamic_slice_in_dim(g, q0, NSA_QBLOCK, axis=3).astype(jnp.float32)

        s = jnp.einsum('bhgqd,bhcd->bhgqc', qp_b, kc, preferred_element_type=jnp.float32) * SCALE
        s = jnp.where(cmp_end[None, :] <= tq[:, None], s, -jnp.inf)
        m = jnp.max(s, axis=-1, keepdims=True)
        m = jnp.where(jnp.isfinite(m), m, 0.0)
        p = jnp.exp(s - m)
        den = jnp.sum(p, axis=-1, keepdims=True)
        p = p / jnp.where(den > 0, den, 1.0)
        o_cmp = jnp.einsum('bhgqc,bhcd->bhgqd', p, vc)

        imp = jnp.einsum('bhgqc,cn->bhqn', p, cmp_to_sel)
        cur = (tq // SEL_BLOCK)[:, None]
        forced = (blk == 0) | (blk == cur) | (blk == cur - 1)
        imp = jnp.where(forced, jnp.inf, jnp.where(blk > cur, -jnp.inf, imp))
        _, idx = lax.top_k(imp, n_sel)
        kg = ks_b[bi, hi, idx].reshape(bsz, A_KV_HEADS, NSA_QBLOCK, n_sel * SEL_BLOCK, dh)
        vg = vs_b[bi, hi, idx].reshape(bsz, A_KV_HEADS, NSA_QBLOCK, n_sel * SEL_BLOCK, dh)
        pos = (idx[..., None] * SEL_BLOCK + jnp.arange(SEL_BLOCK)).reshape(
            bsz, A_KV_HEADS, NSA_QBLOCK, n_sel * SEL_BLOCK)
        s = jnp.einsum('bhgqd,bhqkd->bhgqk', qr_b, kg, preferred_element_type=jnp.float32) * SCALE
        s = jnp.where((pos <= tq[:, None])[:, :, None], s, -jnp.inf)
        o_sel = jnp.einsum('bhgqk,bhqkd->bhgqd', jax.nn.softmax(s, axis=-1), vg.astype(jnp.float32))

        kw = lax.dynamic_slice_in_dim(kw_p, q0, NSA_QBLOCK + NSA_WINDOW, axis=2)
        vw = lax.dynamic_slice_in_dim(vw_p, q0, NSA_QBLOCK + NSA_WINDOW, axis=2)
        kpos = q0 - NSA_WINDOW + jnp.arange(NSA_QBLOCK + NSA_WINDOW)
        diff = tq[:, None] - kpos[None, :]
        wmask = (diff >= 0) & (diff < NSA_WINDOW) & (kpos[None, :] >= 0)
        s = jnp.einsum('bhgqd,bhkd->bhgqk', qr_b, kw, preferred_element_type=jnp.float32) * SCALE
        s = jnp.where(wmask, s, -jnp.inf)
        o_win = jnp.einsum('bhgqk,bhkd->bhgqd', jax.nn.softmax(s, axis=-1), vw.astype(jnp.float32))

        return g_b[..., 0:1] * o_cmp + g_b[..., 1:2] * o_sel + g_b[..., 2:3] * o_win

    out = lax.map(query_block, jnp.arange(seq // NSA_QBLOCK))
    out = out.transpose(1, 0, 4, 2, 3, 5)
    return out.reshape(bsz, seq, A_HEADS * dh)


def swa_sink_attention(q, k, v, sinks):
    bsz, seq = q.shape[:2]
    qg = q.reshape(bsz, seq, B_KV_HEADS, B_GROUP, HEAD_DIM).transpose(0, 2, 3, 1, 4)
    sink = sinks.astype(jnp.float32).reshape(B_KV_HEADS, B_GROUP, 1, 1)
    o, _ = banded_attention(qg, k.transpose(0, 2, 1, 3), v.transpose(0, 2, 1, 3), B_WINDOW - 1, sink)
    return o.transpose(0, 3, 1, 2, 4).reshape(bsz, seq, B_HEADS * HEAD_DIM)


def dilated_attention(q, k, v):
    bsz, seq, nh, dh = q.shape
    qt, kt, vt = (t.transpose(0, 2, 1, 3) for t in (q, k, v))
    outs, lses = [], []
    for window, dil in C_PATTERNS:
        sub = seq // dil

        def strided(t):
            return t.reshape(bsz, nh, sub, dil, dh).transpose(0, 1, 3, 2, 4)

        o, lse = banded_attention(strided(qt)[..., None, :, :], strided(kt), strided(vt), window // dil)
        outs.append(o[..., 0, :, :].transpose(0, 1, 3, 2, 4).reshape(bsz, nh, seq, dh))
        lses.append(lse[..., 0, :].transpose(0, 1, 3, 2).reshape(bsz, nh, seq))
    w = jax.nn.softmax(jnp.stack(lses, axis=0), axis=0)
    o = jnp.einsum('pbht,pbhtd->bhtd', w, jnp.stack(outs, axis=0))
    return o.transpose(0, 2, 1, 3).reshape(bsz, seq, nh * dh)


def setup_inputs(seed: int = 0) -> dict:
    key = jax.random.key(seed)
    ks = jax.random.split(key, 17)
    nrm = jax.random.normal
    f32 = jnp.float32
    return {
        "x": nrm(ks[0], (BATCH, SEQ, D_MODEL), f32),
        "attn_norm": 1.0 + 0.02 * nrm(ks[1], (DEPTH, D_MODEL), f32),
        "ffn_norm": 1.0 + 0.02 * nrm(ks[2], (DEPTH, D_MODEL), f32),
        "final_norm": 1.0 + 0.02 * nrm(ks[3], (D_MODEL,), f32),
        "w_in_e": nrm(ks[4], (N_EVEN, D_MODEL, EVEN_IN), f32) * D_MODEL ** -0.5,
        "w_out_e": nrm(ks[5], (N_EVEN, EVEN_MIX, D_MODEL), f32) * EVEN_MIX ** -0.5,
        "cmp_pe_k": 0.02 * nrm(ks[6], (N_EVEN, CMP_BLOCK, HEAD_DIM), f32),
        "cmp_pe_v": 0.02 * nrm(ks[7], (N_EVEN, CMP_BLOCK, HEAD_DIM), f32),
        "cmp_k_w1": nrm(ks[8], (N_EVEN, CMP_BLOCK * HEAD_DIM, CMP_HIDDEN), f32) * (CMP_BLOCK * HEAD_DIM) ** -0.5,
        "cmp_k_w2": nrm(ks[9], (N_EVEN, CMP_HIDDEN, HEAD_DIM), f32) * CMP_HIDDEN ** -0.5,
        "cmp_v_w1": nrm(ks[10], (N_EVEN, CMP_BLOCK * HEAD_DIM, CMP_HIDDEN), f32) * (CMP_BLOCK * HEAD_DIM) ** -0.5,
        "cmp_v_w2": nrm(ks[11], (N_EVEN, CMP_HIDDEN, HEAD_DIM), f32) * CMP_HIDDEN ** -0.5,
        "sinks": 0.5 * nrm(ks[12], (N_EVEN, B_HEADS), f32),
        "w_qkv_o": nrm(ks[13], (N_ODD, D_MODEL, ODD_IN), f32) * D_MODEL ** -0.5,
        "w_out_o": nrm(ks[14], (N_ODD, ODD_MIX, D_MODEL), f32) * ODD_MIX ** -0.5,
        "w_gate_up": nrm(ks[15], (DEPTH, D_MODEL, 2 * FFN_HIDDEN), f32) * D_MODEL ** -0.5,
        "w_down": nrm(ks[16], (DEPTH, FFN_HIDDEN, D_MODEL), f32) * FFN_HIDDEN ** -0.5,
    }


def reference(x, attn_norm, ffn_norm, final_norm, w_in_e, w_out_e, cmp_pe_k, cmp_pe_v,
              cmp_k_w1, cmp_k_w2, cmp_v_w1, cmp_v_w2, sinks, w_qkv_o, w_out_o,
              w_gate_up, w_down):
    bsz, seq, _ = x.shape
    cos, sin = rope_tables(seq, x.dtype)
    for layer in range(DEPTH):
        h = rms_norm(x, attn_norm[layer])
        if layer % 2 == 0:
            i = layer // 2
            z = h @ w_in_e[i]
            o1 = A_Q
            o2 = o1 + 6 * A_KV
            o3 = o2 + A_GATES
            o4 = o3 + B_Q
            aq = z[..., :o1].reshape(bsz, seq, A_HEADS, HEAD_DIM)
            akc, avc, aks, avs, akw, avw = [t.reshape(bsz, seq, A_KV_HEADS, HEAD_DIM)
                                            for t in jnp.split(z[..., o1:o2], 6, axis=-1)]
            gates = jax.nn.sigmoid(z[..., o2:o3].astype(jnp.float32)).reshape(bsz, seq, A_HEADS, N_BRANCH)
            bq = apply_rope(z[..., o3:o4].reshape(bsz, seq, B_HEADS, HEAD_DIM), cos, sin)
            bk, bv = [t.reshape(bsz, seq, B_KV_HEADS, HEAD_DIM) for t in jnp.split(z[..., o4:], 2, axis=-1)]
            bk = apply_rope(bk, cos, sin)
            a_out = nsa_attention(aq, apply_rope(aq, cos, sin), akc, avc,
                                  apply_rope(aks, cos, sin), avs, apply_rope(akw, cos, sin), avw, gates,
                                  cmp_pe_k[i], cmp_pe_v[i], cmp_k_w1[i], cmp_k_w2[i], cmp_v_w1[i], cmp_v_w2[i])
            b_out = swa_sink_attention(bq, bk, bv, sinks[i])
            mix = jnp.concatenate([a_out, b_out], axis=-1).astype(x.dtype)
            x = x + mix @ w_out_e[i]
        else:
            j = layer // 2
            z = h @ w_qkv_o[j]
            q, k, v = [t.reshape(bsz, seq, C_HEADS, HEAD_DIM) for t in jnp.split(z, 3, axis=-1)]
            c_out = dilated_attention(apply_rope(q, cos, sin), apply_rope(k, cos, sin), v)
            x = x + c_out.astype(x.dtype) @ w_out_o[j]
        h = rms_norm(x, ffn_norm[layer])
        gate, up = jnp.split(h @ w_gate_up[layer], 2, axis=-1)
        x = x + (jax.nn.silu(gate) * up) @ w_down[layer]
    return rms_norm(x, final_norm)
```

```python
import functools

import numpy as np
import jax
import jax.numpy as jnp
from jax import lax
from jax.experimental import pallas as pl
from jax.experimental.pallas import tpu as pltpu

F32 = jnp.float32
BF16 = jnp.bfloat16

D_MODEL = 1024
HEAD_DIM = 64
SCALE = HEAD_DIM ** -0.5
ROPE_THETA = 10000.0
NORM_EPS = 1e-5
LANES = 128
SUBLANES = 8

A_HEADS = 8
A_KV_HEADS = 2
A_GROUP = A_HEADS // A_KV_HEADS
CMP_BLOCK = 32
CMP_STRIDE = 16
CMP_HIDDEN = 256
SEL_BLOCK = 64
SEL_TOPK = 16
NSA_WINDOW = 512
N_BRANCH = 3
B_HEADS = 8
B_KV_HEADS = 2
B_GROUP = B_HEADS // B_KV_HEADS
B_WINDOW = 128
C_HEADS = 16
C_PATTERNS = ((128, 1), (512, 4), (2048, 16))
A_Q = A_HEADS * HEAD_DIM
A_KV = A_KV_HEADS * HEAD_DIM
A_GATES = A_HEADS * N_BRANCH
B_Q = B_HEADS * HEAD_DIM
B_KV = B_KV_HEADS * HEAD_DIM
FFN_HIDDEN = 2816

NEG_BIG = -1e30
VMEM_LIMIT = 56 * 1024 * 1024

_NT = (((1,), (1,)), ((), ()))


def _cparams(sem):
    return pltpu.CompilerParams(dimension_semantics=sem, vmem_limit_bytes=VMEM_LIMIT)


def _rms(x, g):
    ms = jnp.mean(x * x, axis=-1, keepdims=True)
    return x * lax.rsqrt(ms + NORM_EPS) * g


def _norm_proj_kernel(x_ref, g_ref, w_ref, rope_ref, *out_refs, segs):
    h = _rms(x_ref[...], g_ref[...]).astype(BF16)
    cos = rope_ref[:, 0:LANES]
    sin_lo = rope_ref[:, LANES:2 * LANES]
    sin_hi = rope_ref[:, 2 * LANES:3 * LANES]
    for (c0, width, outs) in segs:
        z = jnp.dot(h, w_ref[:, c0:c0 + width], preferred_element_type=F32)
        for j in range(width // LANES):
            zc = z[:, j * LANES:(j + 1) * LANES]
            rot = None
            for (oi, oc, rope, scale, act) in outs:
                v = zc
                if rope:
                    if rot is None:
                        rot = (zc * cos + pltpu.roll(zc, LANES - 32, 1) * sin_lo
                               + pltpu.roll(zc, 32, 1) * sin_hi)
                    v = rot
                if scale is not None:
                    v = v * scale
                if act == "sigmoid":
                    v = jax.nn.sigmoid(v)
                o_ref = out_refs[oi]
                o_ref[:, oc + j * LANES:oc + (j + 1) * LANES] = v.astype(o_ref.dtype)


def _norm_proj(x, g, w, rope_tab, segs, out_defs, seq, tm=512):
    n, d = x.shape
    ncols = w.shape[1]
    tiles_per_seq = seq // tm
    kern = functools.partial(_norm_proj_kernel, segs=segs)
    return pl.pallas_call(
        kern,
        grid=(n // tm,),
        in_specs=[
            pl.BlockSpec((tm, d), lambda i: (i, 0)),
            pl.BlockSpec((1, d), lambda i: (0, 0)),
            pl.BlockSpec((d, ncols), lambda i: (0, 0)),
            pl.BlockSpec((tm, 3 * LANES), lambda i: (i % tiles_per_seq, 0)),
        ],
        out_specs=[pl.BlockSpec((tm, wd), lambda i: (i, 0)) for (wd, _) in out_defs],
        out_shape=[jax.ShapeDtypeStruct((n, wd), dt) for (wd, dt) in out_defs],
        compiler_params=_cparams(("parallel",)),
        name="norm_proj",
    )(x, g.reshape(1, d), w, rope_tab)


def _cmp_mlp_kernel(blk_ref, pe_ref, w1_ref, w2_ref, o_ref):
    b = (blk_ref[0, 0] + pe_ref[0]).astype(BF16)
    hid = jnp.dot(b, w1_ref[0], preferred_element_type=F32)
    a = jax.nn.silu(hid).astype(BF16)
    o_ref[0, 0] = jnp.dot(a, w2_ref[0], preferred_element_type=F32)


def _cmp_mlp(blocks, pe, w1, w2):
    two, nbh, nc, width = blocks.shape
    return pl.pallas_call(
        _cmp_mlp_kernel,
        grid=(two, nbh),
        in_specs=[
            pl.BlockSpec((1, 1, nc, width), lambda s, j: (s, j, 0, 0)),
            pl.BlockSpec((1, 1, width), lambda s, j: (s, 0, 0)),
            pl.BlockSpec((1, width, CMP_HIDDEN), lambda s, j: (s, 0, 0)),
            pl.BlockSpec((1, CMP_HIDDEN, HEAD_DIM), lambda s, j: (s, 0, 0)),
        ],
        out_specs=pl.BlockSpec((1, 1, nc, HEAD_DIM), lambda s, j: (s, j, 0, 0)),
        out_shape=jax.ShapeDtypeStruct((two, nbh, nc, HEAD_DIM), F32),
        compiler_params=_cparams(("parallel", "parallel")),
        name="cmp_mlp",
    )(blocks, pe, w1, w2)


def _nsa_kernel(qp_ref, qr_ref, gate_ref, kc_ref, vc_ref, c2s_ref, ksa_ref, vs_ref,
                kw_ref, vw_ref, o_ref, *, tq, tk):
    kvh = pl.program_id(1)
    i = pl.program_id(2)
    q0 = i * tq
    ncp = kc_ref.shape[2]
    nsel = ksa_ref.shape[2] // SEL_BLOCK
    G = A_GROUP

    kc = kc_ref[0, 0].astype(BF16)
    vc = vc_ref[0, 0].astype(BF16)
    tpos_c = q0 + lax.broadcasted_iota(jnp.int32, (tq, ncp), 0)
    cidx = lax.broadcasted_iota(jnp.int32, (tq, ncp), 1)
    cvalid = (cidx * CMP_STRIDE + (CMP_BLOCK - 1) <= tpos_c) & (cidx < ncp - 1)
    psum = jnp.zeros((tq, ncp), F32)
    o_cmp = []
    for g in range(G):
        qg = qp_ref[0, :, g * HEAD_DIM:(g + 1) * HEAD_DIM]
        s = lax.dot_general(qg, kc, _NT, preferred_element_type=F32)
        s = jnp.where(cvalid, s, -jnp.inf)
        m = jnp.max(s, axis=-1, keepdims=True)
        m = jnp.where(m == -jnp.inf, 0.0, m)
        p = jnp.exp(s - m)
        den = jnp.sum(p, axis=-1, keepdims=True)
        p = p / jnp.where(den > 0, den, 1.0)
        o_cmp.append(jnp.dot(p.astype(BF16), vc, preferred_element_type=F32))
        psum = psum + p

    p_hi = psum.astype(BF16)
    p_lo = (psum - p_hi.astype(F32)).astype(BF16)
    c2s = c2s_ref[...]
    imp = (jnp.dot(p_hi, c2s, preferred_element_type=F32)
           + jnp.dot(p_lo, c2s, preferred_element_type=F32))
    imp_t = imp.T
    blk = lax.broadcasted_iota(jnp.int32, (LANES, tq), 0)
    cur = jnp.right_shift(q0 + lax.broadcasted_iota(jnp.int32, (LANES, tq), 1),
                          SEL_BLOCK.bit_length() - 1)
    forced = (blk == 0) | (blk == cur) | (blk == cur - 1)
    imp_t = jnp.where(forced, jnp.inf, jnp.where(blk > cur, -jnp.inf, imp_t))
    ngrp = nsel // SUBLANES
    imp_g = [imp_t[r * SUBLANES:(r + 1) * SUBLANES] for r in range(ngrp)]
    blk_g = lax.broadcasted_iota(jnp.int32, (SUBLANES, tq), 0)
    cnt_g = [jnp.zeros((SUBLANES, tq), jnp.int32) for _ in range(ngrp)]
    for mblk in range(nsel):
        row = imp_t[mblk:mblk + 1, :]
        for r in range(ngrp):
            lo = r * SUBLANES
            if lo > mblk:
                inc = (row >= imp_g[r]).astype(jnp.int32)
            elif lo + SUBLANES - 1 <= mblk:
                inc = (row > imp_g[r]).astype(jnp.int32)
            else:
                inc = jnp.where(blk_g + lo > mblk, (row >= imp_g[r]).astype(jnp.int32),
                                (row > imp_g[r]).astype(jnp.int32))
            cnt_g[r] = cnt_g[r] + inc
    bias_t = jnp.concatenate(
        [jnp.where(c < SEL_TOPK, 0.0, NEG_BIG).astype(F32) for c in cnt_g]
        + [jnp.zeros((LANES - nsel, tq), F32)], axis=0)
    bias = bias_t.T[:, :nsel].astype(BF16)

    q_heads = [qr_ref[0, :, g * HEAD_DIM:(g + 1) * HEAD_DIM] for g in range(G)]
    qs_aug = jnp.concatenate(
        [jnp.concatenate([qh, bias], axis=1) for qh in q_heads], axis=0)
    rq = q0 + (lax.broadcasted_iota(jnp.int32, (G * tq, tk), 0) & (tq - 1))
    ck = lax.broadcasted_iota(jnp.int32, (G * tq, tk), 1)
    nchunks = (q0 + tq + tk - 1) // tk

    def sel_body(c, carry):
        m, l, acc = carry
        off = pl.multiple_of(c * tk, tk)
        k = ksa_ref[0, 0, pl.ds(off, tk), :]
        v = vs_ref[0, 0, pl.ds(off, tk), :]
        s = lax.dot_general(qs_aug, k, _NT, preferred_element_type=F32)
        s = jnp.where(ck + off <= rq, s, NEG_BIG)
        m_new = jnp.maximum(m, jnp.max(s, axis=-1, keepdims=True))
        alpha = jnp.exp(m - m_new)
        p = jnp.exp(s - m_new)
        l = alpha * l + jnp.sum(p, axis=-1, keepdims=True)
        acc = alpha * acc + jnp.dot(p.astype(BF16), v, preferred_element_type=F32)
        return m_new, l, acc

    m0 = jnp.full((G * tq, 1), -jnp.inf, F32)
    l0 = jnp.zeros((G * tq, 1), F32)
    a0 = jnp.zeros((G * tq, HEAD_DIM), F32)
    _, l_sel, acc_sel = lax.fori_loop(0, nchunks, sel_body, (m0, l0, a0))
    o_sel = acc_sel / l_sel

    nkw = NSA_WINDOW + tq
    start = pl.multiple_of(jnp.maximum(q0 - NSA_WINDOW, 0), tq)
    kw = kw_ref[0, 0, pl.ds(start, nkw), :]
    vw = vw_ref[0, 0, pl.ds(start, nkw), :]
    qs = jnp.concatenate(q_heads, axis=0)
    s = lax.dot_general(qs, kw, _NT, preferred_element_type=F32)
    rq_w = q0 + (lax.broadcasted_iota(jnp.int32, (G * tq, nkw), 0) & (tq - 1))
    kpos = start + lax.broadcasted_iota(jnp.int32, (G * tq, nkw), 1)
    diff = rq_w - kpos
    s = jnp.where((diff >= 0) & (diff < NSA_WINDOW), s, -jnp.inf)
    m = jnp.max(s, axis=-1, keepdims=True)
    p = jnp.exp(s - m)
    den = jnp.sum(p, axis=-1, keepdims=True)
    o_win = jnp.dot(p.astype(BF16), vw, preferred_element_type=F32) / den

    gates = gate_ref[0]
    for g in range(G):
        def gate_col(br):
            c_a = g * N_BRANCH + br
            c_b = (G + g) * N_BRANCH + br
            return jnp.where(kvh == 0, gates[:, c_a:c_a + 1], gates[:, c_b:c_b + 1])
        out = (gate_col(0) * o_cmp[g]
               + gate_col(1) * o_sel[g * tq:(g + 1) * tq]
               + gate_col(2) * o_win[g * tq:(g + 1) * tq])
        o_ref[0, :, g * HEAD_DIM:(g + 1) * HEAD_DIM] = out.astype(o_ref.dtype)


def _nsa(qp, qr, gates, kc, vc, c2s, ksa, vs, kw, vw, tq=128, tk=512):
    bsz, seq, _ = qp.shape
    ncp = kc.shape[2]
    gq = A_GROUP * HEAD_DIM
    kern = functools.partial(_nsa_kernel, tq=tq, tk=tk)
    res4 = lambda b, h, i: (b, h, 0, 0)
    return pl.pallas_call(
        kern,
        grid=(bsz, A_KV_HEADS, seq // tq),
        in_specs=[
            pl.BlockSpec((1, tq, gq), lambda b, h, i: (b, i, h)),
            pl.BlockSpec((1, tq, gq), lambda b, h, i: (b, i, h)),
            pl.BlockSpec((1, tq, LANES), lambda b, h, i: (b, i, 0)),
            pl.BlockSpec((1, 1, ncp, HEAD_DIM), res4),
            pl.BlockSpec((1, 1, ncp, HEAD_DIM), res4),
            pl.BlockSpec((ncp, LANES), lambda b, h, i: (0, 0)),
            pl.BlockSpec((1, 1, seq, LANES), res4),
            pl.BlockSpec((1, 1, seq, HEAD_DIM), res4),
            pl.BlockSpec((1, 1, seq, HEAD_DIM), res4),
            pl.BlockSpec((1, 1, seq, HEAD_DIM), res4),
        ],
        out_specs=pl.BlockSpec((1, tq, gq), lambda b, h, i: (b, i, h)),
        out_shape=jax.ShapeDtypeStruct((bsz, seq, A_Q), BF16),
        compiler_params=_cparams(("parallel", "parallel", "arbitrary")),
        name="nsa",
    )(qp, qr, gates, kc, vc, c2s, ksa, vs, kw, vw)


def _banded_kernel(*refs, heads, group, reach, nprev, tq, has_sink, emit_lse):
    q_ref = refs[0]
    k_refs = refs[1:2 + nprev]
    v_refs = refs[2 + nprev:3 + 2 * nprev]
    pos = 3 + 2 * nprev
    sink_ref = None
    if has_sink:
        sink_ref = refs[pos]
        pos += 1
    o_ref = refs[pos]
    lse_ref = refs[pos + 1] if emit_lse else None

    i = pl.program_id(2)
    nk = (nprev + 1) * tq
    row = lax.broadcasted_iota(jnp.int32, (tq, nk), 0)
    col = lax.broadcasted_iota(jnp.int32, (tq, nk), 1)
    diff = nprev * tq + row - col
    mask = (diff >= 0) & (diff <= reach) & (col >= (nprev - i) * tq)
    if emit_lse:
        lane = lax.broadcasted_iota(jnp.int32, (tq, LANES), 1)
        stats = jnp.zeros((tq, LANES), F32)
    for kvh in range(heads // group):
        ksl = slice(kvh * HEAD_DIM, (kvh + 1) * HEAD_DIM)
        kcat = jnp.concatenate([kr[0, :, ksl] for kr in k_refs], axis=0)
        vcat = jnp.concatenate([vr[0, :, ksl] for vr in v_refs], axis=0)
        for g in range(group):
            h = kvh * group + g
            qh = q_ref[0, :, h * HEAD_DIM:(h + 1) * HEAD_DIM]
            s = lax.dot_general(qh, kcat, _NT, preferred_element_type=F32)
            s = jnp.where(mask, s, -jnp.inf)
            m = jnp.max(s, axis=-1, keepdims=True)
            if has_sink:
                sink = sink_ref[0:1, h:h + 1]
                m = jnp.maximum(m, sink)
            p = jnp.exp(s - m)
            den = jnp.sum(p, axis=-1, keepdims=True)
            if has_sink:
                den = den + jnp.exp(sink - m)
            o = jnp.dot(p.astype(BF16), vcat, preferred_element_type=F32) / den
            o_ref[0, :, h * HEAD_DIM:(h + 1) * HEAD_DIM] = o.astype(o_ref.dtype)
            if emit_lse:
                stats = jnp.where(lane == h, m + jnp.log(den), stats)
    if emit_lse:
        lse_ref[0] = stats


def _banded(q, k, v, *, heads, group, reach, nres, sink=None, emit_lse=False, tq=128):
    bsz, length, _ = q.shape
    cq = heads * HEAD_DIM
    ck = (heads // group) * HEAD_DIM
    nprev = -(-reach // tq)
    kern = functools.partial(_banded_kernel, heads=heads, group=group, reach=reach,
                             nprev=nprev, tq=tq, has_sink=sink is not None, emit_lse=emit_lse)

    def kv_spec(j):
        return pl.BlockSpec((1, tq, ck), lambda b, r, i: (b, jnp.maximum(i - j, 0), r))

    kv_specs = [kv_spec(j) for j in range(nprev, -1, -1)]
    in_specs = [pl.BlockSpec((1, tq, cq), lambda b, r, i: (b, i, r))] + kv_specs + kv_specs
    args = [q] + [k] * (nprev + 1) + [v] * (nprev + 1)
    if sink is not None:
        in_specs.append(pl.BlockSpec((1, LANES), lambda b, r, i: (0, 0)))
        args.append(sink)
    out_specs = [pl.BlockSpec((1, tq, cq), lambda b, r, i: (b, i, r))]
    out_shape = [jax.ShapeDtypeStruct(q.shape, BF16)]
    if emit_lse:
        out_specs.append(pl.BlockSpec((1, tq, LANES), lambda b, r, i: (b, i, r)))
        out_shape.append(jax.ShapeDtypeStruct((bsz, length, nres * LANES), F32))
    return pl.pallas_call(
        kern,
        grid=(bsz, nres, length // tq),
        in_specs=in_specs,
        out_specs=out_specs,
        out_shape=out_shape,
        compiler_params=_cparams(("parallel", "parallel", "arbitrary")),
        name="banded",
    )(*args)


def _outproj2_kernel(x_ref, a_ref, b_ref, w_ref, o_ref):
    mix = jnp.concatenate([a_ref[...], b_ref[...]], axis=-1)
    o_ref[...] = x_ref[...] + jnp.dot(mix, w_ref[...], preferred_element_type=F32)


def _outproj2(x, a, b, w, tm=512):
    n, d = x.shape
    row = lambda wd: pl.BlockSpec((tm, wd), lambda i: (i, 0))
    return pl.pallas_call(
        _outproj2_kernel,
        grid=(n // tm,),
        in_specs=[row(d), row(a.shape[1]), row(b.shape[1]),
                  pl.BlockSpec(w.shape, lambda i: (0, 0))],
        out_specs=row(d),
        out_shape=jax.ShapeDtypeStruct((n, d), F32),
        compiler_params=_cparams(("parallel",)),
        name="outproj2",
    )(x, a, b, w)


def _outproj_merge_kernel(x_ref, o1_ref, o2_ref, o3_ref, l1_ref, l2_ref, l3_ref,
                          e_ref, w_ref, o_ref):
    lses = [l1_ref[...], l2_ref[...], l3_ref[...]]
    m = jnp.maximum(jnp.maximum(lses[0], lses[1]), lses[2])
    es = [jnp.exp(l - m) for l in lses]
    den = es[0] + es[1] + es[2]
    e = e_ref[...]
    mix = None
    for ex, o_r in zip(es, (o1_ref, o2_ref, o3_ref)):
        wgt = ex / den
        w_hi = wgt.astype(BF16)
        w_lo = (wgt - w_hi.astype(F32)).astype(BF16)
        w_exp = (jnp.dot(w_hi, e, preferred_element_type=F32)
                 + jnp.dot(w_lo, e, preferred_element_type=F32))
        term = w_exp * o_r[...].astype(F32)
        mix = term if mix is None else mix + term
    o_ref[...] = x_ref[...] + jnp.dot(mix.astype(BF16), w_ref[...], preferred_element_type=F32)


def _outproj_merge(x, outs, lses, expand, w, tm=512):
    n, d = x.shape
    row = lambda wd: pl.BlockSpec((tm, wd), lambda i: (i, 0))
    return pl.pallas_call(
        _outproj_merge_kernel,
        grid=(n // tm,),
        in_specs=[row(d)] + [row(d)] * 3 + [row(LANES)] * 3
                 + [pl.BlockSpec(expand.shape, lambda i: (0, 0)),
                    pl.BlockSpec(w.shape, lambda i: (0, 0))],
        out_specs=row(d),
        out_shape=jax.ShapeDtypeStruct((n, d), F32),
        compiler_params=_cparams(("parallel",)),
        name="outproj_merge",
    )(x, *outs, *lses, expand, w)


def _ffn_kernel(x_ref, g_ref, wgu_ref, wd_ref, fg_ref, o_ref, acc_ref, *, nchunk, final):
    x = x_ref[...]
    h = _rms(x, g_ref[...]).astype(BF16)
    acc_ref[...] = jnp.zeros_like(acc_ref)

    def body(c, carry):
        gate = jnp.dot(h, wgu_ref[c], preferred_element_type=F32)
        up = jnp.dot(h, wgu_ref[nchunk + c], preferred_element_type=F32)
        a = (jax.nn.silu(gate) * up).astype(BF16)
        acc_ref[...] += jnp.dot(a, wd_ref[c], preferred_element_type=F32)
        return carry

    lax.fori_loop(0, nchunk, body, 0)
    y = x + acc_ref[...]
    if final:
        y = _rms(y, fg_ref[...])
    o_ref[...] = y


def _ffn(x, g, wgu3, wd3, fg, final, tm=512):
    n, d = x.shape
    nchunk = wd3.shape[0]
    kern = functools.partial(_ffn_kernel, nchunk=nchunk, final=final)
    return pl.pallas_call(
        kern,
        grid=(n // tm,),
        in_specs=[
            pl.BlockSpec((tm, d), lambda i: (i, 0)),
            pl.BlockSpec((1, d), lambda i: (0, 0)),
            pl.BlockSpec(wgu3.shape, lambda i: (0, 0, 0)),
            pl.BlockSpec(wd3.shape, lambda i: (0, 0, 0)),
            pl.BlockSpec((1, d), lambda i: (0, 0)),
        ],
        out_specs=pl.BlockSpec((tm, d), lambda i: (i, 0)),
        out_shape=jax.ShapeDtypeStruct((n, d), F32),
        scratch_shapes=[pltpu.VMEM((tm, d), F32)],
        compiler_params=_cparams(("parallel",)),
        name="ffn",
    )(x, g.reshape(1, d), wgu3, wd3, fg.reshape(1, d))


def _rope_table(seq):
    inv = 1.0 / (ROPE_THETA ** (jnp.arange(0, HEAD_DIM, 2, dtype=F32) / HEAD_DIM))
    ang = jnp.arange(seq, dtype=F32)[:, None] * inv[None, :]
    ang = jnp.concatenate([ang, ang], axis=-1)
    cos, sin = jnp.cos(ang), jnp.sin(ang)
    first = (jnp.arange(HEAD_DIM) < HEAD_DIM // 2)[None, :]
    sin_lo = jnp.where(first, -sin, 0.0)
    sin_hi = jnp.where(first, 0.0, sin)
    tile2 = lambda t: jnp.concatenate([t, t], axis=-1)
    return jnp.concatenate([tile2(cos), tile2(sin_lo), tile2(sin_hi)], axis=-1).astype(F32)


def _cmp_to_sel(ncp, nsel):
    c_start = np.arange(ncp) * CMP_STRIDE
    s_start = np.arange(LANES) * SEL_BLOCK
    overlap = ((c_start[:, None] <= s_start[None, :] + SEL_BLOCK - 1)
               & (c_start[:, None] + CMP_BLOCK - 1 >= s_start[None, :]))
    overlap &= (np.arange(ncp) < ncp - 1)[:, None] & (np.arange(LANES) < nsel)[None, :]
    return jnp.asarray(overlap.astype(np.float32), dtype=BF16)


def _ffn_weights(w_gate_up, w_down, th=256):
    d = w_gate_up.shape[0]
    nchunk = FFN_HIDDEN // th
    wgu3 = w_gate_up.astype(BF16).reshape(d, 2 * nchunk, th).transpose(1, 0, 2)
    wd3 = w_down.astype(BF16).reshape(nchunk, th, d)
    return wgu3, wd3


def _heads_first(t, bsz, seq, nh):
    return t.reshape(bsz, seq, nh, HEAD_DIM).transpose(0, 2, 1, 3)


def kernel(x, attn_norm, ffn_norm, final_norm, w_in_e, w_out_e, cmp_pe_k, cmp_pe_v,
           cmp_k_w1, cmp_k_w2, cmp_v_w1, cmp_v_w2, sinks, w_qkv_o, w_out_o,
           w_gate_up, w_down):
    bsz, seq, d = x.shape
    n = bsz * seq
    xf = x.reshape(n, d)
    rope_tab = _rope_table(seq)

    w = w_in_e[0]
    o1 = A_Q
    o2 = o1 + 6 * A_KV
    o3 = o2 + A_GATES
    o4 = o3 + B_Q
    kv = lambda j: w[:, o1 + j * A_KV:o1 + (j + 1) * A_KV]
    gate_w = jnp.pad(w[:, o2:o3], ((0, 0), (0, LANES - A_GATES)))
    w0 = jnp.concatenate(
        [w[:, :o1], kv(2), kv(4), w[:, o3:o4], w[:, o4:o4 + B_KV],
         kv(0), kv(1), kv(3), kv(5), w[:, o4 + B_KV:], gate_w], axis=1).astype(BF16)
    segs0 = (
        (0, 512, ((0, 0, False, SCALE, None), (1, 0, True, SCALE, None))),
        (512, 128, ((2, 0, True, None, None),)),
        (640, 128, ((3, 0, True, None, None),)),
        (768, 512, ((4, 0, True, SCALE, None),)),
        (1280, 128, ((5, 0, True, None, None),)),
        (1408, 256, ((6, 0, False, None, None),)),
        (1664, 128, ((7, 0, False, None, None),)),
        (1792, 128, ((8, 0, False, None, None),)),
        (1920, 128, ((9, 0, False, None, None),)),
        (2048, 128, ((10, 0, False, None, "sigmoid"),)),
    )
    outs0 = ((512, BF16), (512, BF16), (128, BF16), (128, BF16), (512, BF16), (128, BF16),
             (256, F32), (128, BF16), (128, BF16), (128, BF16), (128, F32))
    (qa_p, qa_r, ks_r, kw_r, qb_r, kb_r, kcvc, vs, vw, vb, gates) = _norm_proj(
        xf, attn_norm[0], w0, rope_tab, segs0, outs0, seq)

    nchunks = seq // CMP_STRIDE
    ch = kcvc.reshape(bsz, nchunks, CMP_STRIDE, 2, A_KV_HEADS, HEAD_DIM)
    ch = ch.transpose(3, 0, 4, 1, 2, 5).reshape(2, bsz * A_KV_HEADS, nchunks, CMP_STRIDE * HEAD_DIM)
    blocks = jnp.concatenate([ch[:, :, :-1], ch[:, :, 1:]], axis=-1)
    blocks = jnp.pad(blocks, ((0, 0), (0, 0), (0, 1), (0, 0)))
    pe = jnp.stack([cmp_pe_k[0], cmp_pe_v[0]]).reshape(2, 1, CMP_BLOCK * HEAD_DIM)
    w1 = jnp.stack([cmp_k_w1[0], cmp_v_w1[0]]).astype(BF16)
    w2 = jnp.stack([cmp_k_w2[0], cmp_v_w2[0]]).astype(BF16)
    kcvc_c = _cmp_mlp(blocks, pe, w1, w2).reshape(2, bsz, A_KV_HEADS, nchunks, HEAD_DIM)

    nsel = seq // SEL_BLOCK
    onehot = (jnp.arange(seq)[:, None] // SEL_BLOCK == jnp.arange(nsel)[None, :]).astype(BF16)
    ksa = jnp.concatenate(
        [_heads_first(ks_r, bsz, seq, A_KV_HEADS),
         jnp.broadcast_to(onehot, (bsz, A_KV_HEADS, seq, nsel))], axis=-1)
    a_out = _nsa(qa_p.reshape(bsz, seq, A_Q), qa_r.reshape(bsz, seq, A_Q),
                 gates.reshape(bsz, seq, LANES), kcvc_c[0], kcvc_c[1],
                 _cmp_to_sel(nchunks, nsel), ksa,
                 _heads_first(vs, bsz, seq, A_KV_HEADS),
                 _heads_first(kw_r, bsz, seq, A_KV_HEADS),
                 _heads_first(vw, bsz, seq, A_KV_HEADS))

    sink_row = jnp.pad(sinks[0].astype(F32), (0, LANES - B_HEADS)).reshape(1, LANES)
    (b_out,) = _banded(qb_r.reshape(bsz, seq, B_Q), kb_r.reshape(bsz, seq, B_KV),
                       vb.reshape(bsz, seq, B_KV), heads=B_HEADS, group=B_GROUP,
                       reach=B_WINDOW - 1, nres=1, sink=sink_row)

    xf = _outproj2(xf, a_out.reshape(n, A_Q), b_out.reshape(n, B_Q), w_out_e[0].astype(BF16))
    wgu3, wd3 = _ffn_weights(w_gate_up[0], w_down[0])
    xf = _ffn(xf, ffn_norm[0], wgu3, wd3, final_norm, final=False)

    cw = C_HEADS * HEAD_DIM
    segs1 = (
        (0, cw, ((0, 0, True, SCALE, None),)),
        (cw, cw, ((1, 0, True, None, None),)),
        (2 * cw, cw, ((2, 0, False, None, None),)),
    )
    outs1 = ((cw, BF16), (cw, BF16), (cw, BF16))
    q1, k1, v1 = _norm_proj(xf, attn_norm[1], w_qkv_o[0].astype(BF16), rope_tab, segs1, outs1, seq)
    pat_o, pat_l = [], []
    for window, dil in C_PATTERNS:
        view = lambda t: t.reshape(bsz, seq // dil, dil * cw)
        o_p, l_p = _banded(view(q1), view(k1), view(v1), heads=C_HEADS, group=1,
                           reach=window // dil, nres=dil, emit_lse=True)
        pat_o.append(o_p.reshape(n, cw))
        pat_l.append(l_p.reshape(n, LANES))
    expand = (jnp.arange(LANES)[:, None] == jnp.arange(cw)[None, :] // HEAD_DIM).astype(BF16)
    xf = _outproj_merge(xf, pat_o, pat_l, expand, w_out_o[0].astype(BF16))
    wgu3, wd3 = _ffn_weights(w_gate_up[1], w_down[1])
    xf = _ffn(xf, ffn_norm[1], wgu3, wd3, final_norm, final=True)
    return xf.reshape(bsz, seq, d)
```

```python
import functools

import numpy as np
import jax
import jax.numpy as jnp
from jax import lax
from jax.experimental import pallas as pl
from jax.experimental.pallas import tpu as pltpu

F32 = jnp.float32
BF16 = jnp.bfloat16

D_MODEL = 1024
HEAD_DIM = 64
SCALE = HEAD_DIM ** -0.5
ROPE_THETA = 10000.0
NORM_EPS = 1e-5
LANES = 128
SUBLANES = 8

A_HEADS = 8
A_KV_HEADS = 2
A_GROUP = A_HEADS // A_KV_HEADS
CMP_BLOCK = 32
CMP_STRIDE = 16
CMP_HIDDEN = 256
SEL_BLOCK = 64
SEL_TOPK = 16
NSA_WINDOW = 512
N_BRANCH = 3
B_HEADS = 8
B_KV_HEADS = 2
B_GROUP = B_HEADS // B_KV_HEADS
B_WINDOW = 128
C_HEADS = 16
C_PATTERNS = ((128, 1), (512, 4), (2048, 16))
A_Q = A_HEADS * HEAD_DIM
A_KV = A_KV_HEADS * HEAD_DIM
A_GATES = A_HEADS * N_BRANCH
B_Q = B_HEADS * HEAD_DIM
B_KV = B_KV_HEADS * HEAD_DIM
C_Q = C_HEADS * HEAD_DIM
FFN_HIDDEN = 2816

NEG_BIG = -1e30
VMEM_LIMIT = 56 * 1024 * 1024

_NT = (((1,), (1,)), ((), ()))


def _cparams(sem):
    return pltpu.CompilerParams(dimension_semantics=sem, vmem_limit_bytes=VMEM_LIMIT)


def _rms(x, g):
    ms = jnp.mean(x * x, axis=-1, keepdims=True)
    return x * lax.rsqrt(ms + NORM_EPS) * g


def _low_half(rows):
    return lax.broadcasted_iota(jnp.int32, (rows, LANES), 1) < HEAD_DIM


def _rope(zc, tab_ref):
    cos = tab_ref[:, 0:LANES]
    sin_lo = tab_ref[:, LANES:2 * LANES]
    sin_hi = tab_ref[:, 2 * LANES:3 * LANES]
    return zc * cos + pltpu.roll(zc, LANES - 32, 1) * sin_lo + pltpu.roll(zc, 32, 1) * sin_hi


def _masked_pairs(pair, lo):
    zero = jnp.zeros_like(pair)
    return jnp.where(lo, pair, zero), jnp.where(lo, zero, pair)


def _proj0_kernel(x_ref, g_ref, w_ref, tab_ref, qp_ref, qr_ref, ksa_ref, kwd_ref, qb_ref,
                  kbd_ref, kcn_ref, vcn_ref, vsa_ref, vwa_ref, vbd_ref, gate_ref):
    tm = x_ref.shape[1]
    h = _rms(x_ref[0], g_ref[...]).astype(BF16)
    lo = _low_half(tm)
    onehot = tab_ref[:, 3 * LANES:4 * LANES]

    def mm(c0, width):
        return jnp.dot(h, w_ref[:, c0:c0 + width], preferred_element_type=F32)

    def cols(z, j):
        return z[:, j * LANES:(j + 1) * LANES]

    def dup(v):
        sw = pltpu.roll(v, HEAD_DIM, 1)
        return jnp.where(lo, v, sw), jnp.where(lo, sw, v)

    def aug(v):
        sw = pltpu.roll(v, HEAD_DIM, 1)
        return jnp.where(lo, v, 1.0), jnp.where(lo, sw, 1.0)

    def put(ref, j, v):
        ref[0, :, j * LANES:(j + 1) * LANES] = v.astype(ref.dtype)

    z = mm(0, A_Q)
    for j in range(A_Q // LANES):
        zc = cols(z, j)
        put(qp_ref, j, zc * SCALE)
        put(qr_ref, j, _rope(zc, tab_ref) * SCALE)
    z = mm(A_Q, 2 * LANES)
    d0, d1 = dup(_rope(cols(z, 0), tab_ref))
    put(ksa_ref, 0, d0)
    put(ksa_ref, 1, onehot)
    put(ksa_ref, 2, d1)
    put(ksa_ref, 3, onehot)
    d0, d1 = dup(_rope(cols(z, 1), tab_ref))
    put(kwd_ref, 0, d0)
    put(kwd_ref, 1, d1)
    z = mm(A_Q + 2 * LANES, B_Q)
    for j in range(B_Q // LANES):
        put(qb_ref, j, _rope(cols(z, j), tab_ref) * SCALE)
    z = mm(A_Q + 2 * LANES + B_Q, 7 * LANES)
    d0, d1 = dup(_rope(cols(z, 0), tab_ref))
    put(kbd_ref, 0, d0)
    put(kbd_ref, 1, d1)
    put(kcn_ref, 0, cols(z, 1))
    put(vcn_ref, 0, cols(z, 2))
    a0, a1 = aug(cols(z, 3))
    put(vsa_ref, 0, a0)
    put(vsa_ref, 1, a1)
    a0, a1 = aug(cols(z, 4))
    put(vwa_ref, 0, a0)
    put(vwa_ref, 1, a1)
    d0, d1 = dup(cols(z, 5))
    put(vbd_ref, 0, d0)
    put(vbd_ref, 1, d1)
    put(gate_ref, 0, jax.nn.sigmoid(cols(z, 6)))


def _proj0(x, g, w, tab, tm=512):
    bsz, seq, d = x.shape
    outs = ((A_Q, BF16), (A_Q, BF16), (4 * LANES, BF16), (2 * LANES, BF16), (B_Q, BF16),
            (2 * LANES, BF16), (LANES, F32), (LANES, F32), (2 * LANES, BF16), (2 * LANES, BF16),
            (2 * LANES, BF16), (LANES, F32))
    tile = lambda wd: pl.BlockSpec((1, tm, wd), lambda b, j: (b, j, 0))
    return pl.pallas_call(
        _proj0_kernel,
        grid=(bsz, seq // tm),
        in_specs=[tile(d),
                  pl.BlockSpec((1, d), lambda b, j: (0, 0)),
                  pl.BlockSpec(w.shape, lambda b, j: (0, 0)),
                  pl.BlockSpec((tm, 4 * LANES), lambda b, j: (j, 0))],
        out_specs=[tile(wd) for wd, _ in outs],
        out_shape=[jax.ShapeDtypeStruct((bsz, seq, wd), dt) for wd, dt in outs],
        compiler_params=_cparams(("parallel", "parallel")),
        name="proj0",
    )(x, g.reshape(1, d), w, tab)


def _proj1_kernel(x_ref, g_ref, w_ref, tab_ref, *refs):
    out_refs = refs[:9]
    sc_ref = refs[9]
    tm = x_ref.shape[1]
    h = _rms(x_ref[0], g_ref[...]).astype(BF16)
    ncb = C_Q // LANES
    for seg in range(3):
        nat_ref, d4_ref, d16_ref = out_refs[3 * seg:3 * seg + 3]
        z = jnp.dot(h, w_ref[:, seg * C_Q:(seg + 1) * C_Q], preferred_element_type=F32)
        for cb in range(ncb):
            cs = slice(cb * LANES, (cb + 1) * LANES)
            zc = z[:, cs]
            if seg < 2:
                zc = _rope(zc, tab_ref)
            if seg == 0:
                zc = zc * SCALE
            nat_ref[0, :, cs] = zc.astype(BF16)
            sc_ref[cb] = zc
        for cb in range(ncb):
            cs = slice(cb * LANES, (cb + 1) * LANES)
            for r in range(4):
                d4_ref[0, r, :, cs] = sc_ref[cb, pl.ds(r, tm // 4, stride=4), :].astype(BF16)
            for r in range(16):
                d16_ref[0, r, :, cs] = sc_ref[cb, pl.ds(r, tm // 16, stride=16), :].astype(BF16)


def _proj1(x, g, w, tab, tm=512):
    bsz, seq, d = x.shape
    tile = lambda wd: pl.BlockSpec((1, tm, wd), lambda b, j: (b, j, 0))
    dil = lambda dd: pl.BlockSpec((1, dd, tm // dd, C_Q), lambda b, j: (b, 0, j, 0))
    out_specs, out_shape = [], []
    for _ in range(3):
        out_specs += [tile(C_Q), dil(4), dil(16)]
        out_shape += [jax.ShapeDtypeStruct((bsz, seq, C_Q), BF16),
                      jax.ShapeDtypeStruct((bsz, 4, seq // 4, C_Q), BF16),
                      jax.ShapeDtypeStruct((bsz, 16, seq // 16, C_Q), BF16)]
    return pl.pallas_call(
        _proj1_kernel,
        grid=(bsz, seq // tm),
        in_specs=[tile(d),
                  pl.BlockSpec((1, d), lambda b, j: (0, 0)),
                  pl.BlockSpec(w.shape, lambda b, j: (0, 0)),
                  pl.BlockSpec((tm, 4 * LANES), lambda b, j: (j, 0))],
        out_specs=out_specs,
        out_shape=out_shape,
        scratch_shapes=[pltpu.VMEM((C_Q // LANES, tm, LANES), F32)],
        compiler_params=_cparams(("parallel", "parallel")),
        name="proj1",
    )(x, g.reshape(1, d), w, tab)


def _cmp_mlp_kernel(kc_ref, vc_ref, pe_ref, w1_ref, w2_ref, o_ref):
    nc = o_ref.shape[2]
    for s, src in enumerate((kc_ref, vc_ref)):
        hid_a = jnp.zeros((nc, 2 * CMP_HIDDEN), F32)
        hid_b = jnp.zeros((nc, 2 * CMP_HIDDEN), F32)
        for p in range(CMP_STRIDE):
            xp = src[0, pl.ds(p, nc, stride=CMP_STRIDE), :]
            xa = (xp + pe_ref[s, p:p + 1, :]).astype(BF16)
            xb = (xp + pe_ref[s, CMP_STRIDE + p:CMP_STRIDE + p + 1, :]).astype(BF16)
            hid_a = hid_a + jnp.dot(xa, w1_ref[s, p], preferred_element_type=F32)
            hid_b = hid_b + jnp.dot(xb, w1_ref[s, CMP_STRIDE + p], preferred_element_type=F32)
        hid = hid_a + pltpu.roll(hid_b, nc - 1, 0)
        a = jax.nn.silu(hid).astype(BF16)
        for hh in range(A_KV_HEADS):
            o_ref[0, 2 * s + hh] = jnp.dot(a[:, hh * CMP_HIDDEN:(hh + 1) * CMP_HIDDEN], w2_ref[s],
                                           preferred_element_type=F32)


def _cmp_mlp(kcn, vcn, pe2, w1bd, w2):
    bsz, seq, _ = kcn.shape
    nc = seq // CMP_STRIDE
    nat = pl.BlockSpec((1, seq, LANES), lambda b: (b, 0, 0))
    return pl.pallas_call(
        _cmp_mlp_kernel,
        grid=(bsz,),
        in_specs=[nat, nat,
                  pl.BlockSpec(pe2.shape, lambda b: (0, 0, 0)),
                  pl.BlockSpec(w1bd.shape, lambda b: (0, 0, 0, 0)),
                  pl.BlockSpec(w2.shape, lambda b: (0, 0, 0))],
        out_specs=pl.BlockSpec((1, 2 * A_KV_HEADS, nc, HEAD_DIM), lambda b: (b, 0, 0, 0)),
        out_shape=jax.ShapeDtypeStruct((bsz, 2 * A_KV_HEADS, nc, HEAD_DIM), F32),
        compiler_params=_cparams(("parallel",)),
        name="cmp_mlp",
    )(kcn, vcn, pe2, w1bd, w2)


def _nsa_kernel(qp_ref, qr_ref, gate_ref, kc_ref, vc_ref, c2s_ref, ksa_ref, vsa_ref,
                kwd_ref, vwa_ref, o_ref, *, tq, tk):
    kvh = pl.program_id(1)
    i = pl.program_id(2)
    q0 = i * tq
    ncp = kc_ref.shape[2]
    nsel = ksa_ref.shape[1] // SEL_BLOCK
    G = A_GROUP
    rows = G * tq
    lo = _low_half(tq)

    def masked_heads(ref):
        out = []
        for pr in range(G // 2):
            out.extend(_masked_pairs(ref[0, :, pr * LANES:(pr + 1) * LANES], lo))
        return out

    kc = kc_ref[0, 0].astype(BF16)
    vc = vc_ref[0, 0].astype(BF16)
    kc2 = jnp.concatenate([kc, kc], axis=1)
    vc2 = jnp.concatenate([vc, vc], axis=1)
    tpos_c = q0 + lax.broadcasted_iota(jnp.int32, (tq, ncp), 0)
    cidx = lax.broadcasted_iota(jnp.int32, (tq, ncp), 1)
    cvalid = (cidx * CMP_STRIDE + (CMP_BLOCK - 1) <= tpos_c) & (cidx < ncp - 1)
    cbias = jnp.where(cvalid, 0.0, -jnp.inf).astype(F32)
    qc = jnp.concatenate(masked_heads(qp_ref), axis=0)
    s = lax.dot_general(qc, kc2, _NT, preferred_element_type=F32)
    s = s + jnp.concatenate([cbias] * G, axis=0)
    m = jnp.max(s, axis=-1, keepdims=True)
    m = jnp.where(m == -jnp.inf, 0.0, m)
    p = jnp.exp(s - m)
    den = jnp.sum(p, axis=-1, keepdims=True)
    p = p / jnp.where(den > 0, den, 1.0)
    o_cmp = jnp.dot(p.astype(BF16), vc2, preferred_element_type=F32)
    psum = p[0:tq]
    for g in range(1, G):
        psum = psum + p[g * tq:(g + 1) * tq]

    p_hi = psum.astype(BF16)
    p_lo = (psum - p_hi.astype(F32)).astype(BF16)
    c2s = c2s_ref[...]
    imp = (jnp.dot(p_hi, c2s, preferred_element_type=F32)
           + jnp.dot(p_lo, c2s, preferred_element_type=F32))
    imp_t = imp.T
    blk = lax.broadcasted_iota(jnp.int32, (LANES, tq), 0)
    cur = jnp.right_shift(q0 + lax.broadcasted_iota(jnp.int32, (LANES, tq), 1),
                          SEL_BLOCK.bit_length() - 1)
    forced = (blk == 0) | (blk == cur) | (blk == cur - 1)
    imp_t = jnp.where(forced, jnp.inf, jnp.where(blk > cur, -jnp.inf, imp_t))
    ngrp = nsel // SUBLANES
    imp_g = [imp_t[r * SUBLANES:(r + 1) * SUBLANES] for r in range(ngrp)]
    blk_g = lax.broadcasted_iota(jnp.int32, (SUBLANES, tq), 0)
    cnt_g = [jnp.zeros((SUBLANES, tq), jnp.int32) for _ in range(ngrp)]
    for mblk in range(nsel):
        row = imp_t[mblk:mblk + 1, :]
        for r in range(ngrp):
            first = r * SUBLANES
            if first > mblk:
                inc = (row >= imp_g[r]).astype(jnp.int32)
            elif first + SUBLANES - 1 <= mblk:
                inc = (row > imp_g[r]).astype(jnp.int32)
            else:
                inc = jnp.where(blk_g + first > mblk, (row >= imp_g[r]).astype(jnp.int32),
                                (row > imp_g[r]).astype(jnp.int32))
            cnt_g[r] = cnt_g[r] + inc
    bias_t = jnp.concatenate(
        [jnp.where(c < SEL_TOPK, 0.0, NEG_BIG).astype(F32) for c in cnt_g]
        + [jnp.zeros((LANES - nsel, tq), F32)], axis=0)
    bias = bias_t.T.astype(BF16)

    q_heads = masked_heads(qr_ref)
    qs_aug = jnp.concatenate(
        [jnp.concatenate([qh, bias], axis=1) for qh in q_heads], axis=0)

    def sel_chunk(c, m, acc, diagonal):
        off = pl.multiple_of(c * tk, tk)
        k = ksa_ref[0, pl.ds(off, tk), :]
        v = vsa_ref[0, pl.ds(off, tk), :]
        s = lax.dot_general(qs_aug, k, _NT, preferred_element_type=F32)
        if diagonal:
            rq = q0 + (lax.broadcasted_iota(jnp.int32, (rows, tk), 0) & (tq - 1))
            ck = off + lax.broadcasted_iota(jnp.int32, (rows, tk), 1)
            s = jnp.where(ck <= rq, s, NEG_BIG)
        m_new = jnp.maximum(m, jnp.max(s, axis=-1, keepdims=True))
        alpha = jnp.exp(m - m_new)
        p = jnp.exp(s - m_new)
        acc = alpha * acc + jnp.dot(p.astype(BF16), v, preferred_element_type=F32)
        return m_new, acc

    n_full = q0 // tk
    m0 = jnp.full((rows, 1), -jnp.inf, F32)
    a0 = jnp.zeros((rows, LANES), F32)
    m_f, acc_f = lax.fori_loop(0, n_full, lambda c, carry: sel_chunk(c, *carry, False), (m0, a0))
    _, acc_sel = sel_chunk(n_full, m_f, acc_f, True)

    nkw = NSA_WINDOW + tq
    start = pl.multiple_of(jnp.maximum(q0 - NSA_WINDOW, 0), tq)
    kw = kwd_ref[0, pl.ds(start, nkw), :]
    vw = vwa_ref[0, pl.ds(start, nkw), :]
    qs = jnp.concatenate(q_heads, axis=0)
    s = lax.dot_general(qs, kw, _NT, preferred_element_type=F32)
    rq_w = q0 + lax.broadcasted_iota(jnp.int32, (tq, nkw), 0)
    kpos = start + lax.broadcasted_iota(jnp.int32, (tq, nkw), 1)
    diff = rq_w - kpos
    wbias = jnp.where((diff >= 0) & (diff < NSA_WINDOW), 0.0, -jnp.inf).astype(F32)
    s = s + jnp.concatenate([wbias] * G, axis=0)
    m = jnp.max(s, axis=-1, keepdims=True)
    p = jnp.exp(s - m)
    acc_win = jnp.dot(p.astype(BF16), vw, preferred_element_type=F32)

    sel_sw = pltpu.roll(acc_sel, HEAD_DIM, 1)
    win_sw = pltpu.roll(acc_win, HEAD_DIM, 1)
    gates = gate_ref[0]
    for pr in range(G // 2):
        halves = []
        for hh in range(2):
            g = 2 * pr + hh
            rs = slice(g * tq, (g + 1) * tq)
            if hh == 0:
                o_s = acc_sel[rs] / sel_sw[rs]
                o_w = acc_win[rs] / win_sw[rs]
            else:
                o_s = sel_sw[rs] / acc_sel[rs]
                o_w = win_sw[rs] / acc_win[rs]

            def gate_col(br, g=g):
                c_a = g * N_BRANCH + br
                c_b = (G + g) * N_BRANCH + br
                return jnp.where(kvh == 0, gates[:, c_a:c_a + 1], gates[:, c_b:c_b + 1])
            halves.append(gate_col(0) * o_cmp[rs] + gate_col(1) * o_s + gate_col(2) * o_w)
        o_ref[0, :, pr * LANES:(pr + 1) * LANES] = jnp.where(lo, halves[0], halves[1]).astype(o_ref.dtype)


def _nsa(qp, qr, gates, kcvc, c2s, ksa, vsa, kwd, vwa, tq=128, tk=512):
    bsz, seq, _ = qp.shape
    ncp = kcvc.shape[2]
    gq = A_GROUP * HEAD_DIM
    kern = functools.partial(_nsa_kernel, tq=tq, tk=tk)
    qtile = pl.BlockSpec((1, tq, gq), lambda b, h, i: (b, i, h))
    res = lambda wd: pl.BlockSpec((1, seq, wd), lambda b, h, i: (b, 0, h))
    return pl.pallas_call(
        kern,
        grid=(bsz, A_KV_HEADS, seq // tq),
        in_specs=[
            qtile, qtile,
            pl.BlockSpec((1, tq, LANES), lambda b, h, i: (b, i, 0)),
            pl.BlockSpec((1, 1, ncp, HEAD_DIM), lambda b, h, i: (b, h, 0, 0)),
            pl.BlockSpec((1, 1, ncp, HEAD_DIM), lambda b, h, i: (b, A_KV_HEADS + h, 0, 0)),
            pl.BlockSpec((ncp, LANES), lambda b, h, i: (0, 0)),
            res(2 * LANES), res(LANES), res(LANES), res(LANES),
        ],
        out_specs=qtile,
        out_shape=jax.ShapeDtypeStruct((bsz, seq, A_Q), BF16),
        compiler_params=_cparams(("parallel", "parallel", "arbitrary")),
        name="nsa",
    )(qp, qr, gates, kcvc, kcvc, c2s, ksa, vsa, kwd, vwa)


def _banded_kernel(*refs, npairs, pairs_per_kv, reach, tq, nsub, has_sink, emit_lse):
    q_ref, kp_ref, kc_ref, vp_ref, vc_ref = refs[:5]
    pos = 5
    sink_ref = None
    if has_sink:
        sink_ref = refs[pos]
        pos += 1
    o_ref = refs[pos]
    lse_ref = refs[pos + 1] if emit_lse else None

    i = pl.program_id(2)
    nk = 2 * tq
    lo = _low_half(tq)
    lane = lax.broadcasted_iota(jnp.int32, (tq, LANES), 1)
    row = lax.broadcasted_iota(jnp.int32, (tq, nk), 0)
    col = lax.broadcasted_iota(jnp.int32, (tq, nk), 1)
    diff = tq + row - col
    band = (diff >= 0) & (diff <= reach)
    bias_band = jnp.where(band, 0.0, -jnp.inf).astype(F32)
    bias_first = jnp.where(band & ((col >= tq) | (i > 0)), 0.0, -jnp.inf).astype(F32)
    bias2 = [jnp.concatenate([b, b], axis=0) for b in (bias_first, bias_band)]

    nkv = npairs // pairs_per_kv
    k_all = [jnp.concatenate([kp_ref[0, 0, :, c * LANES:(c + 1) * LANES],
                              kc_ref[0, 0, :, c * LANES:(c + 1) * LANES]], axis=0) for c in range(nkv)]
    v_all = [jnp.concatenate([vp_ref[0, 0, :, c * LANES:(c + 1) * LANES],
                              vc_ref[0, 0, :, c * LANES:(c + 1) * LANES]], axis=0) for c in range(nkv)]

    parts = []
    for j in range(nsub):
        for p in range(npairs):
            qpair = q_ref[0, 0, j * tq:(j + 1) * tq, p * LANES:(p + 1) * LANES]
            q2 = jnp.concatenate(_masked_pairs(qpair, lo), axis=0)
            kk = k_all[p // pairs_per_kv][j * tq:(j + 2) * tq]
            sp = lax.dot_general(q2, kk, _NT, preferred_element_type=F32)
            parts.append(sp + bias2[min(j, 1)])
    s = jnp.concatenate(parts, axis=0)
    m = jnp.max(s, axis=-1, keepdims=True)
    if has_sink:
        sink = jnp.concatenate(
            [jnp.broadcast_to(sink_ref[0:1, 2 * p + hh:2 * p + hh + 1], (tq, 1))
             for j in range(nsub) for p in range(npairs) for hh in range(2)], axis=0)
        m = jnp.maximum(m, sink)
    pexp = jnp.exp(s - m)
    den = jnp.sum(pexp, axis=-1, keepdims=True)
    if has_sink:
        den = den + jnp.exp(sink - m)
    pb = pexp.astype(BF16)
    if emit_lse:
        lse = m + jnp.log(den)
    for j in range(nsub):
        stats = jnp.zeros((tq, LANES), F32)
        for p in range(npairs):
            part = j * npairs + p
            rs = slice(part * 2 * tq, (part + 1) * 2 * tq)
            vv = v_all[p // pairs_per_kv][j * tq:(j + 2) * tq]
            r = jnp.dot(pb[rs], vv, preferred_element_type=F32) / den[rs]
            o_ref[0, 0, j * tq:(j + 1) * tq, p * LANES:(p + 1) * LANES] = (
                jnp.where(lo, r[:tq], r[tq:]).astype(o_ref.dtype))
            if emit_lse:
                for hh in range(2):
                    first = part * 2 * tq + hh * tq
                    stats = jnp.where(lane == 2 * p + hh, lse[first:first + tq], stats)
        if emit_lse:
            lse_ref[0, 0, j * tq:(j + 1) * tq, :] = stats


def _banded(q, k, v, *, pairs_per_kv, reach, nsub, sink=None, emit_lse=False, tq=128):
    bsz, nres, length, cq = q.shape
    ck = k.shape[-1]
    npairs = cq // LANES
    assert reach <= tq and length % (nsub * tq) == 0
    kern = functools.partial(_banded_kernel, npairs=npairs, pairs_per_kv=pairs_per_kv, reach=reach,
                             tq=tq, nsub=nsub, has_sink=sink is not None, emit_lse=emit_lse)
    cur = lambda wd: pl.BlockSpec((1, 1, nsub * tq, wd), lambda b, r, i: (b, r, i, 0))
    prev = lambda wd: pl.BlockSpec((1, 1, tq, wd),
                                   lambda b, r, i: (b, r, jnp.maximum(i * nsub - 1, 0), 0))
    in_specs = [cur(cq), prev(ck), cur(ck), prev(ck), cur(ck)]
    args = [q, k, k, v, v]
    if sink is not None:
        in_specs.append(pl.BlockSpec((1, LANES), lambda b, r, i: (0, 0)))
        args.append(sink)
    out_specs = [cur(cq)]
    out_shape = [jax.ShapeDtypeStruct(q.shape, BF16)]
    if emit_lse:
        out_specs.append(cur(LANES))
        out_shape.append(jax.ShapeDtypeStruct((bsz, nres, length, LANES), F32))
    return pl.pallas_call(
        kern,
        grid=(bsz, nres, length // (nsub * tq)),
        in_specs=in_specs,
        out_specs=out_specs,
        out_shape=out_shape,
        compiler_params=_cparams(("parallel", "parallel", "arbitrary")),
        name="banded",
    )(*args)


def _outproj2_kernel(x_ref, a_ref, b_ref, w_ref, o_ref):
    mix = jnp.concatenate([a_ref[...], b_ref[...]], axis=-1)
    o_ref[...] = x_ref[...] + jnp.dot(mix, w_ref[...], preferred_element_type=F32)


def _outproj2(x, a, b, w, tm=512):
    n, d = x.shape
    row = lambda wd: pl.BlockSpec((tm, wd), lambda i: (i, 0))
    return pl.pallas_call(
        _outproj2_kernel,
        grid=(n // tm,),
        in_specs=[row(d), row(a.shape[1]), row(b.shape[1]),
                  pl.BlockSpec(w.shape, lambda i: (0, 0))],
        out_specs=row(d),
        out_shape=jax.ShapeDtypeStruct((n, d), F32),
        compiler_params=_cparams(("parallel",)),
        name="outproj2",
    )(x, a, b, w)


def _outproj_merge_kernel(x_ref, o1_ref, o4_ref, o16_ref, l1_ref, l4_ref, l16_ref, e_ref, w_ref,
                          o_ref, sl4_ref, sl16_ref, so4_ref, so16_ref):
    tm = x_ref.shape[1]
    for r in range(4):
        sl4_ref[pl.ds(r, tm // 4, stride=4), :] = l4_ref[0, r]
    for r in range(16):
        sl16_ref[pl.ds(r, tm // 16, stride=16), :] = l16_ref[0, r]
    ncb = C_Q // LANES
    for cb in range(ncb):
        cs = slice(cb * LANES, (cb + 1) * LANES)
        for r in range(4):
            so4_ref[cb, pl.ds(r, tm // 4, stride=4), :] = o4_ref[0, r, :, cs].astype(F32)
        for r in range(16):
            so16_ref[cb, pl.ds(r, tm // 16, stride=16), :] = o16_ref[0, r, :, cs].astype(F32)
    lses = [l1_ref[0], sl4_ref[...], sl16_ref[...]]
    m = jnp.maximum(jnp.maximum(lses[0], lses[1]), lses[2])
    es = [jnp.exp(l - m) for l in lses]
    den = es[0] + es[1] + es[2]
    e = e_ref[...]
    w_exp = []
    for ex in es:
        wgt = ex / den
        w_hi = wgt.astype(BF16)
        w_lo = (wgt - w_hi.astype(F32)).astype(BF16)
        w_exp.append(jnp.dot(w_hi, e, preferred_element_type=F32)
                     + jnp.dot(w_lo, e, preferred_element_type=F32))
    mix = []
    for cb in range(ncb):
        cs = slice(cb * LANES, (cb + 1) * LANES)
        mix.append((w_exp[0][:, cs] * o1_ref[0, :, cs].astype(F32)
                    + w_exp[1][:, cs] * so4_ref[cb]
                    + w_exp[2][:, cs] * so16_ref[cb]).astype(BF16))
    o_ref[0] = x_ref[0] + jnp.dot(jnp.concatenate(mix, axis=-1), w_ref[...],
                                  preferred_element_type=F32)


def _outproj_merge(x, outs, lses, expand, w, tm=512):
    bsz, seq, d = x.shape
    tile = lambda wd: pl.BlockSpec((1, tm, wd), lambda b, j: (b, j, 0))
    dil = lambda dd, wd: pl.BlockSpec((1, dd, tm // dd, wd), lambda b, j: (b, 0, j, 0))
    return pl.pallas_call(
        _outproj_merge_kernel,
        grid=(bsz, seq // tm),
        in_specs=[tile(d), tile(C_Q), dil(4, C_Q), dil(16, C_Q),
                  tile(LANES), dil(4, LANES), dil(16, LANES),
                  pl.BlockSpec(expand.shape, lambda b, j: (0, 0)),
                  pl.BlockSpec(w.shape, lambda b, j: (0, 0))],
        out_specs=tile(d),
        out_shape=jax.ShapeDtypeStruct((bsz, seq, d), F32),
        scratch_shapes=[pltpu.VMEM((tm, LANES), F32), pltpu.VMEM((tm, LANES), F32),
                        pltpu.VMEM((C_Q // LANES, tm, LANES), F32),
                        pltpu.VMEM((C_Q // LANES, tm, LANES), F32)],
        compiler_params=_cparams(("parallel", "parallel")),
        name="outproj_merge",
    )(x, *outs, *lses, expand, w)


def _ffn_kernel(x_ref, g_ref, wgu_ref, wd_ref, fg_ref, o_ref, acc_ref, *, nchunk, final):
    x = x_ref[...]
    h = _rms(x, g_ref[...]).astype(BF16)
    acc_ref[...] = jnp.zeros_like(acc_ref)

    def body(c, carry):
        gate = jnp.dot(h, wgu_ref[c], preferred_element_type=F32)
        up = jnp.dot(h, wgu_ref[nchunk + c], preferred_element_type=F32)
        a = (jax.nn.silu(gate) * up).astype(BF16)
        acc_ref[...] += jnp.dot(a, wd_ref[c], preferred_element_type=F32)
        return carry

    lax.fori_loop(0, nchunk, body, 0)
    y = x + acc_ref[...]
    if final:
        y = _rms(y, fg_ref[...])
    o_ref[...] = y


def _ffn(x, g, wgu3, wd3, fg, final, tm=512):
    n, d = x.shape
    nchunk = wd3.shape[0]
    kern = functools.partial(_ffn_kernel, nchunk=nchunk, final=final)
    return pl.pallas_call(
        kern,
        grid=(n // tm,),
        in_specs=[
            pl.BlockSpec((tm, d), lambda i: (i, 0)),
            pl.BlockSpec((1, d), lambda i: (0, 0)),
            pl.BlockSpec(wgu3.shape, lambda i: (0, 0, 0)),
            pl.BlockSpec(wd3.shape, lambda i: (0, 0, 0)),
            pl.BlockSpec((1, d), lambda i: (0, 0)),
        ],
        out_specs=pl.BlockSpec((tm, d), lambda i: (i, 0)),
        out_shape=jax.ShapeDtypeStruct((n, d), F32),
        scratch_shapes=[pltpu.VMEM((tm, d), F32)],
        compiler_params=_cparams(("parallel",)),
        name="ffn",
    )(x, g.reshape(1, d), wgu3, wd3, fg.reshape(1, d))


def _tables(seq):
    inv = 1.0 / (ROPE_THETA ** (jnp.arange(0, HEAD_DIM, 2, dtype=F32) / HEAD_DIM))
    ang = jnp.arange(seq, dtype=F32)[:, None] * inv[None, :]
    ang = jnp.concatenate([ang, ang], axis=-1)
    cos, sin = jnp.cos(ang), jnp.sin(ang)
    first = (jnp.arange(HEAD_DIM) < HEAD_DIM // 2)[None, :]
    sin_lo = jnp.where(first, -sin, 0.0)
    sin_hi = jnp.where(first, 0.0, sin)
    tile2 = lambda t: jnp.concatenate([t, t], axis=-1)
    onehot = (jnp.arange(seq)[:, None] // SEL_BLOCK == jnp.arange(LANES)[None, :]).astype(F32)
    return jnp.concatenate([tile2(cos), tile2(sin_lo), tile2(sin_hi), onehot], axis=-1).astype(F32)


def _cmp_to_sel(ncp, nsel):
    c_start = np.arange(ncp) * CMP_STRIDE
    s_start = np.arange(LANES) * SEL_BLOCK
    overlap = ((c_start[:, None] <= s_start[None, :] + SEL_BLOCK - 1)
               & (c_start[:, None] + CMP_BLOCK - 1 >= s_start[None, :]))
    overlap &= (np.arange(ncp) < ncp - 1)[:, None] & (np.arange(LANES) < nsel)[None, :]
    return jnp.asarray(overlap.astype(np.float32), dtype=BF16)


def _cmp_weights(pe_k, pe_v, k_w1, v_w1, k_w2, v_w2):
    pe2 = jnp.stack([jnp.concatenate([pe, pe], axis=-1) for pe in (pe_k, pe_v)])
    w1 = jnp.stack([k_w1, v_w1]).astype(BF16).reshape(2, CMP_BLOCK, HEAD_DIM, CMP_HIDDEN)
    zero = jnp.zeros_like(w1)
    w1bd = jnp.concatenate([jnp.concatenate([w1, zero], axis=-1),
                            jnp.concatenate([zero, w1], axis=-1)], axis=-2)
    w2 = jnp.stack([k_w2, v_w2]).astype(BF16)
    return pe2, w1bd, w2


def _ffn_weights(w_gate_up, w_down, th=256):
    d = w_gate_up.shape[0]
    nchunk = FFN_HIDDEN // th
    wgu3 = w_gate_up.astype(BF16).reshape(d, 2 * nchunk, th).transpose(1, 0, 2)
    wd3 = w_down.astype(BF16).reshape(nchunk, th, d)
    return wgu3, wd3


def kernel(x, attn_norm, ffn_norm, final_norm, w_in_e, w_out_e, cmp_pe_k, cmp_pe_v,
           cmp_k_w1, cmp_k_w2, cmp_v_w1, cmp_v_w2, sinks, w_qkv_o, w_out_o,
           w_gate_up, w_down):
    bsz, seq, d = x.shape
    n = bsz * seq
    tab = _tables(seq)

    w = w_in_e[0]
    o1 = A_Q
    o2 = o1 + 6 * A_KV
    o3 = o2 + A_GATES
    o4 = o3 + B_Q
    kv = lambda j: w[:, o1 + j * A_KV:o1 + (j + 1) * A_KV]
    gate_w = jnp.pad(w[:, o2:o3], ((0, 0), (0, LANES - A_GATES)))
    w0 = jnp.concatenate(
        [w[:, :o1], kv(2), kv(4), w[:, o3:o4], w[:, o4:o4 + B_KV],
         kv(0), kv(1), kv(3), kv(5), w[:, o4 + B_KV:], gate_w], axis=1).astype(BF16)
    (qa_p, qa_r, ksa, kwd, qb_r, kbd, kcn, vcn, vsa, vwa, vbd, gates) = _proj0(
        x, attn_norm[0], w0, tab)

    pe2, w1bd, w2 = _cmp_weights(cmp_pe_k[0], cmp_pe_v[0], cmp_k_w1[0], cmp_v_w1[0],
                                 cmp_k_w2[0], cmp_v_w2[0])
    kcvc = _cmp_mlp(kcn, vcn, pe2, w1bd, w2)

    a_out = _nsa(qa_p, qa_r, gates, kcvc, _cmp_to_sel(seq // CMP_STRIDE, seq // SEL_BLOCK),
                 ksa, vsa, kwd, vwa)

    sink_row = jnp.pad(sinks[0].astype(F32), (0, LANES - B_HEADS)).reshape(1, LANES)
    r4 = lambda t: t.reshape(bsz, 1, seq, t.shape[-1])
    (b_out,) = _banded(r4(qb_r), r4(kbd), r4(vbd), pairs_per_kv=B_GROUP // 2,
                       reach=B_WINDOW - 1, nsub=2, sink=sink_row)

    xf = _outproj2(x.reshape(n, d), a_out.reshape(n, A_Q), b_out.reshape(n, B_Q),
                   w_out_e[0].astype(BF16))
    wgu3, wd3 = _ffn_weights(w_gate_up[0], w_down[0])
    xf = _ffn(xf, ffn_norm[0], wgu3, wd3, final_norm, final=False)

    x1 = xf.reshape(bsz, seq, d)
    (q_n, q_4, q_16, k_n, k_4, k_16, v_n, v_4, v_16) = _proj1(
        x1, attn_norm[1], w_qkv_o[0].astype(BF16), tab)
    pat_o, pat_l = [], []
    for (window, dil), (qq, kk, vv) in zip(
            C_PATTERNS, ((r4(q_n), r4(k_n), r4(v_n)), (q_4, k_4, v_4), (q_16, k_16, v_16))):
        o_p, l_p = _banded(qq, kk, vv, pairs_per_kv=1, reach=window // dil,
                           nsub=1, emit_lse=True)
        pat_o.append(o_p)
        pat_l.append(l_p)
    pat_o[0] = pat_o[0].reshape(bsz, seq, C_Q)
    pat_l[0] = pat_l[0].reshape(bsz, seq, LANES)
    expand = (jnp.arange(LANES)[:, None] == jnp.arange(C_Q)[None, :] // HEAD_DIM).astype(BF16)
    x2 = _outproj_merge(x1, pat_o, pat_l, expand, w_out_o[0].astype(BF16))
    wgu3, wd3 = _ffn_weights(w_gate_up[1], w_down[1])
    xf = _ffn(x2.reshape(n, d), ffn_norm[1], wgu3, wd3, final_norm, final=True)
    return xf.reshape(bsz, seq, d)
```

```python
import functools

import numpy as np
import jax
import jax.numpy as jnp
from jax import lax
from jax.experimental import pallas as pl
from jax.experimental.pallas import tpu as pltpu

F32 = jnp.float32
BF16 = jnp.bfloat16

D_MODEL = 1024
HEAD_DIM = 64
SCALE = HEAD_DIM ** -0.5
ROPE_THETA = 10000.0
NORM_EPS = 1e-5
LANES = 128
SUBLANES = 8

A_HEADS = 8
A_KV_HEADS = 2
A_GROUP = A_HEADS // A_KV_HEADS
CMP_BLOCK = 32
CMP_STRIDE = 16
CMP_HIDDEN = 256
SEL_BLOCK = 64
SEL_TOPK = 16
NSA_WINDOW = 512
N_BRANCH = 3
B_HEADS = 8
B_KV_HEADS = 2
B_GROUP = B_HEADS // B_KV_HEADS
B_WINDOW = 128
C_HEADS = 16
C_PATTERNS = ((128, 1), (512, 4), (2048, 16))
A_Q = A_HEADS * HEAD_DIM
A_KV = A_KV_HEADS * HEAD_DIM
A_GATES = A_HEADS * N_BRANCH
B_Q = B_HEADS * HEAD_DIM
B_KV = B_KV_HEADS * HEAD_DIM
C_Q = C_HEADS * HEAD_DIM
FFN_HIDDEN = 2816

NEG_BIG = -1e30
VMEM_LIMIT = 56 * 1024 * 1024

_NT = (((1,), (1,)), ((), ()))


def _cparams(sem):
    return pltpu.CompilerParams(dimension_semantics=sem, vmem_limit_bytes=VMEM_LIMIT)


def _rms(x, g):
    ms = jnp.mean(x * x, axis=-1, keepdims=True)
    return x * lax.rsqrt(ms + NORM_EPS) * g


def _low_half(rows):
    return lax.broadcasted_iota(jnp.int32, (rows, LANES), 1) < HEAD_DIM


def _rope(zc, tab_ref):
    cos = tab_ref[:, 0:LANES]
    sin_lo = tab_ref[:, LANES:2 * LANES]
    sin_hi = tab_ref[:, 2 * LANES:3 * LANES]
    return zc * cos + pltpu.roll(zc, LANES - 32, 1) * sin_lo + pltpu.roll(zc, 32, 1) * sin_hi


def _masked_pairs(pair, lo):
    zero = jnp.zeros_like(pair)
    return jnp.where(lo, pair, zero), jnp.where(lo, zero, pair)


def _proj0_kernel(x_ref, g_ref, w_ref, tab_ref, qp_ref, qr_ref, ksa_ref, kwd_ref, qb_ref,
                  kbd_ref, kcn_ref, vcn_ref, vsa_ref, vwa_ref, vbd_ref, gate_ref):
    tm = x_ref.shape[1]
    h = _rms(x_ref[0], g_ref[...]).astype(BF16)
    lo = _low_half(tm)
    onehot = tab_ref[:, 3 * LANES:4 * LANES]

    def mm(c0, width):
        return jnp.dot(h, w_ref[:, c0:c0 + width], preferred_element_type=F32)

    def cols(z, j):
        return z[:, j * LANES:(j + 1) * LANES]

    def dup(v):
        sw = pltpu.roll(v, HEAD_DIM, 1)
        return jnp.where(lo, v, sw), jnp.where(lo, sw, v)

    def aug(v):
        sw = pltpu.roll(v, HEAD_DIM, 1)
        return jnp.where(lo, v, 1.0), jnp.where(lo, sw, 1.0)

    def put(ref, j, v):
        ref[0, :, j * LANES:(j + 1) * LANES] = v.astype(ref.dtype)

    z = mm(0, A_Q)
    for j in range(A_Q // LANES):
        zc = cols(z, j)
        put(qp_ref, j, zc * SCALE)
        put(qr_ref, j, _rope(zc, tab_ref) * SCALE)
    z = mm(A_Q, 2 * LANES)
    d0, d1 = dup(_rope(cols(z, 0), tab_ref))
    put(ksa_ref, 0, d0)
    put(ksa_ref, 1, onehot)
    put(ksa_ref, 2, d1)
    put(ksa_ref, 3, onehot)
    d0, d1 = dup(_rope(cols(z, 1), tab_ref))
    put(kwd_ref, 0, d0)
    put(kwd_ref, 1, d1)
    z = mm(A_Q + 2 * LANES, B_Q)
    for j in range(B_Q // LANES):
        put(qb_ref, j, _rope(cols(z, j), tab_ref) * SCALE)
    z = mm(A_Q + 2 * LANES + B_Q, 7 * LANES)
    d0, d1 = dup(_rope(cols(z, 0), tab_ref))
    put(kbd_ref, 0, d0)
    put(kbd_ref, 1, d1)
    put(kcn_ref, 0, cols(z, 1))
    put(vcn_ref, 0, cols(z, 2))
    a0, a1 = aug(cols(z, 3))
    put(vsa_ref, 0, a0)
    put(vsa_ref, 1, a1)
    a0, a1 = aug(cols(z, 4))
    put(vwa_ref, 0, a0)
    put(vwa_ref, 1, a1)
    d0, d1 = dup(cols(z, 5))
    put(vbd_ref, 0, d0)
    put(vbd_ref, 1, d1)
    put(gate_ref, 0, jax.nn.sigmoid(cols(z, 6)))


def _proj0(x, g, w, tab, tm=512):
    bsz, seq, d = x.shape
    outs = ((A_Q, BF16), (A_Q, BF16), (4 * LANES, BF16), (2 * LANES, BF16), (B_Q, BF16),
            (2 * LANES, BF16), (LANES, F32), (LANES, F32), (2 * LANES, BF16), (2 * LANES, BF16),
            (2 * LANES, BF16), (LANES, F32))
    tile = lambda wd: pl.BlockSpec((1, tm, wd), lambda b, j: (b, j, 0))
    return pl.pallas_call(
        _proj0_kernel,
        grid=(bsz, seq // tm),
        in_specs=[tile(d),
                  pl.BlockSpec((1, d), lambda b, j: (0, 0)),
                  pl.BlockSpec(w.shape, lambda b, j: (0, 0)),
                  pl.BlockSpec((tm, 4 * LANES), lambda b, j: (j, 0))],
        out_specs=[tile(wd) for wd, _ in outs],
        out_shape=[jax.ShapeDtypeStruct((bsz, seq, wd), dt) for wd, dt in outs],
        compiler_params=_cparams(("parallel", "parallel")),
        name="proj0",
    )(x, g.reshape(1, d), w, tab)


def _proj1_kernel(x_ref, g_ref, w_ref, tab_ref, *refs):
    out_refs = refs[:9]
    sc_ref = refs[9]
    tm = x_ref.shape[1]
    h = _rms(x_ref[0], g_ref[...]).astype(BF16)
    ncb = C_Q // LANES
    for seg in range(3):
        nat_ref, d4_ref, d16_ref = out_refs[3 * seg:3 * seg + 3]
        z = jnp.dot(h, w_ref[:, seg * C_Q:(seg + 1) * C_Q], preferred_element_type=F32)
        for cb in range(ncb):
            cs = slice(cb * LANES, (cb + 1) * LANES)
            zc = z[:, cs]
            if seg < 2:
                zc = _rope(zc, tab_ref)
            if seg == 0:
                zc = zc * SCALE
            nat_ref[0, :, cs] = zc.astype(BF16)
            sc_ref[cb] = zc
        for cb in range(ncb):
            cs = slice(cb * LANES, (cb + 1) * LANES)
            for r in range(4):
                d4_ref[0, r, :, cs] = sc_ref[cb, pl.ds(r, tm // 4, stride=4), :].astype(BF16)
            for r in range(16):
                d16_ref[0, r, :, cs] = sc_ref[cb, pl.ds(r, tm // 16, stride=16), :].astype(BF16)


def _proj1(x, g, w, tab, tm=512):
    bsz, seq, d = x.shape
    tile = lambda wd: pl.BlockSpec((1, tm, wd), lambda b, j: (b, j, 0))
    dil = lambda dd: pl.BlockSpec((1, dd, tm // dd, C_Q), lambda b, j: (b, 0, j, 0))
    out_specs, out_shape = [], []
    for _ in range(3):
        out_specs += [tile(C_Q), dil(4), dil(16)]
        out_shape += [jax.ShapeDtypeStruct((bsz, seq, C_Q), BF16),
                      jax.ShapeDtypeStruct((bsz, 4, seq // 4, C_Q), BF16),
                      jax.ShapeDtypeStruct((bsz, 16, seq // 16, C_Q), BF16)]
    return pl.pallas_call(
        _proj1_kernel,
        grid=(bsz, seq // tm),
        in_specs=[tile(d),
                  pl.BlockSpec((1, d), lambda b, j: (0, 0)),
                  pl.BlockSpec(w.shape, lambda b, j: (0, 0)),
                  pl.BlockSpec((tm, 4 * LANES), lambda b, j: (j, 0))],
        out_specs=out_specs,
        out_shape=out_shape,
        scratch_shapes=[pltpu.VMEM((C_Q // LANES, tm, LANES), F32)],
        compiler_params=_cparams(("parallel", "parallel")),
        name="proj1",
    )(x, g.reshape(1, d), w, tab)


def _cmp_mlp_kernel(kc_ref, vc_ref, pe_ref, w1_ref, w2_ref, o_ref):
    nc = o_ref.shape[2]
    for s, src in enumerate((kc_ref, vc_ref)):
        hid_a = jnp.zeros((nc, 2 * CMP_HIDDEN), F32)
        hid_b = jnp.zeros((nc, 2 * CMP_HIDDEN), F32)
        for p in range(CMP_STRIDE):
            xp = src[0, pl.ds(p, nc, stride=CMP_STRIDE), :]
            xa = (xp + pe_ref[s, p:p + 1, :]).astype(BF16)
            xb = (xp + pe_ref[s, CMP_STRIDE + p:CMP_STRIDE + p + 1, :]).astype(BF16)
            hid_a = hid_a + jnp.dot(xa, w1_ref[s, p], preferred_element_type=F32)
            hid_b = hid_b + jnp.dot(xb, w1_ref[s, CMP_STRIDE + p], preferred_element_type=F32)
        hid = hid_a + pltpu.roll(hid_b, nc - 1, 0)
        a = jax.nn.silu(hid).astype(BF16)
        for hh in range(A_KV_HEADS):
            o_ref[0, 2 * s + hh] = jnp.dot(a[:, hh * CMP_HIDDEN:(hh + 1) * CMP_HIDDEN], w2_ref[s],
                                           preferred_element_type=F32)


def _cmp_mlp(kcn, vcn, pe2, w1bd, w2):
    bsz, seq, _ = kcn.shape
    nc = seq // CMP_STRIDE
    nat = pl.BlockSpec((1, seq, LANES), lambda b: (b, 0, 0))
    return pl.pallas_call(
        _cmp_mlp_kernel,
        grid=(bsz,),
        in_specs=[nat, nat,
                  pl.BlockSpec(pe2.shape, lambda b: (0, 0, 0)),
                  pl.BlockSpec(w1bd.shape, lambda b: (0, 0, 0, 0)),
                  pl.BlockSpec(w2.shape, lambda b: (0, 0, 0))],
        out_specs=pl.BlockSpec((1, 2 * A_KV_HEADS, nc, HEAD_DIM), lambda b: (b, 0, 0, 0)),
        out_shape=jax.ShapeDtypeStruct((bsz, 2 * A_KV_HEADS, nc, HEAD_DIM), F32),
        compiler_params=_cparams(("parallel",)),
        name="cmp_mlp",
    )(kcn, vcn, pe2, w1bd, w2)


def _nsa_kernel(qp_ref, qr_ref, gate_ref, kcvc_ref, c2s_ref, ksa_ref, vsa_ref,
                kwd_ref, vwa_ref, o_ref, *, tq, tk):
    i = pl.program_id(1)
    q0 = i * tq
    ncp = kcvc_ref.shape[2]
    nsel = ksa_ref.shape[1] // SEL_BLOCK
    G = A_GROUP
    NH = A_KV_HEADS
    rows = G * tq
    lo = _low_half(tq)
    streams = range(NH)

    def masked_heads(ref, kvh):
        out = []
        for pr in range(G // 2):
            c0 = (kvh * (G // 2) + pr) * LANES
            out.extend(_masked_pairs(ref[0, :, c0:c0 + LANES], lo))
        return out

    tpos_c = q0 + lax.broadcasted_iota(jnp.int32, (tq, ncp), 0)
    cidx = lax.broadcasted_iota(jnp.int32, (tq, ncp), 1)
    cvalid = (cidx * CMP_STRIDE + (CMP_BLOCK - 1) <= tpos_c) & (cidx < ncp - 1)
    cbias = jnp.where(cvalid, 0.0, -jnp.inf).astype(F32)
    cbias4 = jnp.concatenate([cbias] * G, axis=0)
    o_cmp, psum = [], []
    for kvh in streams:
        kc = kcvc_ref[0, kvh].astype(BF16)
        vc = kcvc_ref[0, NH + kvh].astype(BF16)
        kc2 = jnp.concatenate([kc, kc], axis=1)
        vc2 = jnp.concatenate([vc, vc], axis=1)
        qc = jnp.concatenate(masked_heads(qp_ref, kvh), axis=0)
        s = lax.dot_general(qc, kc2, _NT, preferred_element_type=F32) + cbias4
        m = jnp.max(s, axis=-1, keepdims=True)
        m = jnp.where(m == -jnp.inf, 0.0, m)
        p = jnp.exp(s - m)
        den = jnp.sum(p, axis=-1, keepdims=True)
        p = p / jnp.where(den > 0, den, 1.0)
        o_cmp.append(jnp.dot(p.astype(BF16), vc2, preferred_element_type=F32))
        ps = p[0:tq]
        for g in range(1, G):
            ps = ps + p[g * tq:(g + 1) * tq]
        psum.append(ps)

    nkw = NSA_WINDOW + tq
    start = pl.multiple_of(jnp.maximum(q0 - NSA_WINDOW, 0), tq)
    rq_w = q0 + lax.broadcasted_iota(jnp.int32, (tq, nkw), 0)
    kpos = start + lax.broadcasted_iota(jnp.int32, (tq, nkw), 1)
    diff = rq_w - kpos
    wbias = jnp.where((diff >= 0) & (diff < NSA_WINDOW), 0.0, -jnp.inf).astype(F32)
    wbias4 = jnp.concatenate([wbias] * G, axis=0)
    q_heads = [masked_heads(qr_ref, kvh) for kvh in streams]
    acc_win = []
    for kvh in streams:
        kw = kwd_ref[0, pl.ds(start, nkw), kvh * LANES:(kvh + 1) * LANES]
        vw = vwa_ref[0, pl.ds(start, nkw), kvh * LANES:(kvh + 1) * LANES]
        qs = jnp.concatenate(q_heads[kvh], axis=0)
        s = lax.dot_general(qs, kw, _NT, preferred_element_type=F32) + wbias4
        m = jnp.max(s, axis=-1, keepdims=True)
        p = jnp.exp(s - m)
        acc_win.append(jnp.dot(p.astype(BF16), vw, preferred_element_type=F32))

    c2s = c2s_ref[...]
    blk = lax.broadcasted_iota(jnp.int32, (LANES, tq), 0)
    cur = jnp.right_shift(q0 + lax.broadcasted_iota(jnp.int32, (LANES, tq), 1),
                          SEL_BLOCK.bit_length() - 1)
    forced = (blk == 0) | (blk == cur) | (blk == cur - 1)
    future = blk > cur
    blk_g = lax.broadcasted_iota(jnp.int32, (SUBLANES, tq), 0)
    ngrp = nsel // SUBLANES
    qs_aug = []
    for kvh in streams:
        p_hi = psum[kvh].astype(BF16)
        p_lo = (psum[kvh] - p_hi.astype(F32)).astype(BF16)
        imp = (jnp.dot(p_hi, c2s, preferred_element_type=F32)
               + jnp.dot(p_lo, c2s, preferred_element_type=F32))
        imp_t = jnp.where(forced, jnp.inf, jnp.where(future, -jnp.inf, imp.T))
        imp_g = [imp_t[r * SUBLANES:(r + 1) * SUBLANES] for r in range(ngrp)]
        cnt_g = [jnp.zeros((SUBLANES, tq), jnp.int32) for _ in range(ngrp)]
        for mblk in range(nsel):
            row = imp_t[mblk:mblk + 1, :]
            for r in range(ngrp):
                first = r * SUBLANES
                if first > mblk:
                    inc = (row >= imp_g[r]).astype(jnp.int32)
                elif first + SUBLANES - 1 <= mblk:
                    inc = (row > imp_g[r]).astype(jnp.int32)
                else:
                    inc = jnp.where(blk_g + first > mblk, (row >= imp_g[r]).astype(jnp.int32),
                                    (row > imp_g[r]).astype(jnp.int32))
                cnt_g[r] = cnt_g[r] + inc
        bias_t = jnp.concatenate(
            [jnp.where(c < SEL_TOPK, 0.0, NEG_BIG).astype(F32) for c in cnt_g]
            + [jnp.zeros((LANES - nsel, tq), F32)], axis=0)
        bias = bias_t.T.astype(BF16)
        qs_aug.append(jnp.concatenate(
            [jnp.concatenate([qh, bias], axis=1) for qh in q_heads[kvh]], axis=0))

    def sel_chunk(c, carry, diagonal):
        off = pl.multiple_of(c * tk, tk)
        out = []
        for kvh in streams:
            m, acc = carry[kvh]
            k = ksa_ref[0, pl.ds(off, tk), kvh * 2 * LANES:(kvh + 1) * 2 * LANES]
            v = vsa_ref[0, pl.ds(off, tk), kvh * LANES:(kvh + 1) * LANES]
            s = lax.dot_general(qs_aug[kvh], k, _NT, preferred_element_type=F32)
            if diagonal:
                rq = q0 + (lax.broadcasted_iota(jnp.int32, (rows, tk), 0) & (tq - 1))
                ck = off + lax.broadcasted_iota(jnp.int32, (rows, tk), 1)
                s = jnp.where(ck <= rq, s, NEG_BIG)
            m_new = jnp.maximum(m, jnp.max(s, axis=-1, keepdims=True))
            alpha = jnp.exp(m - m_new)
            p = jnp.exp(s - m_new)
            acc = alpha * acc + jnp.dot(p.astype(BF16), v, preferred_element_type=F32)
            out.append((m_new, acc))
        return tuple(out)

    n_full = q0 // tk
    init = tuple((jnp.full((rows, 1), -jnp.inf, F32), jnp.zeros((rows, LANES), F32)) for _ in streams)
    carry = lax.fori_loop(0, n_full, lambda c, cr: sel_chunk(c, cr, False), init)
    carry = sel_chunk(n_full, carry, True)

    gates = gate_ref[0]
    for kvh in streams:
        acc_sel = carry[kvh][1]
        sel_sw = pltpu.roll(acc_sel, HEAD_DIM, 1)
        win_sw = pltpu.roll(acc_win[kvh], HEAD_DIM, 1)
        for pr in range(G // 2):
            halves = []
            for hh in range(2):
                g = 2 * pr + hh
                rs = slice(g * tq, (g + 1) * tq)
                if hh == 0:
                    o_s = acc_sel[rs] / sel_sw[rs]
                    o_w = acc_win[kvh][rs] / win_sw[rs]
                else:
                    o_s = sel_sw[rs] / acc_sel[rs]
                    o_w = win_sw[rs] / acc_win[kvh][rs]
                gc = (kvh * G + g) * N_BRANCH
                halves.append(gates[:, gc:gc + 1] * o_cmp[kvh][rs] + gates[:, gc + 1:gc + 2] * o_s
                              + gates[:, gc + 2:gc + 3] * o_w)
            c0 = (kvh * (G // 2) + pr) * LANES
            o_ref[0, :, c0:c0 + LANES] = jnp.where(lo, halves[0], halves[1]).astype(o_ref.dtype)


def _nsa(qp, qr, gates, kcvc, c2s, ksa, vsa, kwd, vwa, tq=128, tk=1024):
    bsz, seq, _ = qp.shape
    ncp = kcvc.shape[2]
    kern = functools.partial(_nsa_kernel, tq=tq, tk=tk)
    qtile = pl.BlockSpec((1, tq, A_Q), lambda b, i: (b, i, 0))
    res = lambda wd: pl.BlockSpec((1, seq, wd), lambda b, i: (b, 0, 0))
    return pl.pallas_call(
        kern,
        grid=(bsz, seq // tq),
        in_specs=[
            qtile, qtile,
            pl.BlockSpec((1, tq, LANES), lambda b, i: (b, i, 0)),
            pl.BlockSpec((1, 2 * A_KV_HEADS, ncp, HEAD_DIM), lambda b, i: (b, 0, 0, 0)),
            pl.BlockSpec((ncp, LANES), lambda b, i: (0, 0)),
            res(4 * LANES), res(2 * LANES), res(2 * LANES), res(2 * LANES),
        ],
        out_specs=qtile,
        out_shape=jax.ShapeDtypeStruct((bsz, seq, A_Q), BF16),
        compiler_params=_cparams(("parallel", "arbitrary")),
        name="nsa",
    )(qp, qr, gates, kcvc, c2s, ksa, vsa, kwd, vwa)


def _banded_kernel(*refs, npairs, pairs_per_kv, reach, tq, nsub, has_sink, emit_lse):
    q_ref, kp_ref, kc_ref, vp_ref, vc_ref = refs[:5]
    pos = 5
    sink_ref = None
    if has_sink:
        sink_ref = refs[pos]
        pos += 1
    o_ref = refs[pos]
    lse_ref = refs[pos + 1] if emit_lse else None

    i = pl.program_id(2)
    nk = 2 * tq
    lo = _low_half(tq)
    lane = lax.broadcasted_iota(jnp.int32, (tq, LANES), 1)
    row = lax.broadcasted_iota(jnp.int32, (tq, nk), 0)
    col = lax.broadcasted_iota(jnp.int32, (tq, nk), 1)
    diff = tq + row - col
    band = (diff >= 0) & (diff <= reach)
    bias_band = jnp.where(band, 0.0, -jnp.inf).astype(F32)
    bias_first = jnp.where(band & ((col >= tq) | (i > 0)), 0.0, -jnp.inf).astype(F32)
    bias2 = [jnp.concatenate([b, b], axis=0) for b in (bias_first, bias_band)]

    nkv = npairs // pairs_per_kv
    k_all = [jnp.concatenate([kp_ref[0, 0, :, c * LANES:(c + 1) * LANES],
                              kc_ref[0, 0, :, c * LANES:(c + 1) * LANES]], axis=0) for c in range(nkv)]
    v_all = [jnp.concatenate([vp_ref[0, 0, :, c * LANES:(c + 1) * LANES],
                              vc_ref[0, 0, :, c * LANES:(c + 1) * LANES]], axis=0) for c in range(nkv)]

    parts = []
    for j in range(nsub):
        for p in range(npairs):
            qpair = q_ref[0, 0, j * tq:(j + 1) * tq, p * LANES:(p + 1) * LANES]
            q2 = jnp.concatenate(_masked_pairs(qpair, lo), axis=0)
            kk = k_all[p // pairs_per_kv][j * tq:(j + 2) * tq]
            sp = lax.dot_general(q2, kk, _NT, preferred_element_type=F32)
            parts.append(sp + bias2[min(j, 1)])
    s = jnp.concatenate(parts, axis=0)
    m = jnp.max(s, axis=-1, keepdims=True)
    if has_sink:
        sink = jnp.concatenate(
            [jnp.broadcast_to(sink_ref[0:1, 2 * p + hh:2 * p + hh + 1], (tq, 1))
             for j in range(nsub) for p in range(npairs) for hh in range(2)], axis=0)
        m = jnp.maximum(m, sink)
    pexp = jnp.exp(s - m)
    den = jnp.sum(pexp, axis=-1, keepdims=True)
    if has_sink:
        den = den + jnp.exp(sink - m)
    pb = pexp.astype(BF16)
    if emit_lse:
        lse = m + jnp.log(den)
    for j in range(nsub):
        stats = jnp.zeros((tq, LANES), F32)
        for p in range(npairs):
            part = j * npairs + p
            rs = slice(part * 2 * tq, (part + 1) * 2 * tq)
            vv = v_all[p // pairs_per_kv][j * tq:(j + 2) * tq]
            r = jnp.dot(pb[rs], vv, preferred_element_type=F32) / den[rs]
            o_ref[0, 0, j * tq:(j + 1) * tq, p * LANES:(p + 1) * LANES] = (
                jnp.where(lo, r[:tq], r[tq:]).astype(o_ref.dtype))
            if emit_lse:
                for hh in range(2):
                    first = part * 2 * tq + hh * tq
                    stats = jnp.where(lane == 2 * p + hh, lse[first:first + tq], stats)
        if emit_lse:
            lse_ref[0, 0, j * tq:(j + 1) * tq, :] = stats


def _banded(q, k, v, *, pairs_per_kv, reach, nsub, sink=None, emit_lse=False, tq=128):
    bsz, nres, length, cq = q.shape
    ck = k.shape[-1]
    npairs = cq // LANES
    assert reach <= tq and length % (nsub * tq) == 0
    kern = functools.partial(_banded_kernel, npairs=npairs, pairs_per_kv=pairs_per_kv, reach=reach,
                             tq=tq, nsub=nsub, has_sink=sink is not None, emit_lse=emit_lse)
    cur = lambda wd: pl.BlockSpec((1, 1, nsub * tq, wd), lambda b, r, i: (b, r, i, 0))
    prev = lambda wd: pl.BlockSpec((1, 1, tq, wd),
                                   lambda b, r, i: (b, r, jnp.maximum(i * nsub - 1, 0), 0))
    in_specs = [cur(cq), prev(ck), cur(ck), prev(ck), cur(ck)]
    args = [q, k, k, v, v]
    if sink is not None:
        in_specs.append(pl.BlockSpec((1, LANES), lambda b, r, i: (0, 0)))
        args.append(sink)
    out_specs = [cur(cq)]
    out_shape = [jax.ShapeDtypeStruct(q.shape, BF16)]
    if emit_lse:
        out_specs.append(cur(LANES))
        out_shape.append(jax.ShapeDtypeStruct((bsz, nres, length, LANES), F32))
    return pl.pallas_call(
        kern,
        grid=(bsz, nres, length // (nsub * tq)),
        in_specs=in_specs,
        out_specs=out_specs,
        out_shape=out_shape,
        compiler_params=_cparams(("parallel", "parallel", "arbitrary")),
        name="banded",
    )(*args)


def _outproj2_kernel(x_ref, a_ref, b_ref, w_ref, o_ref):
    mix = jnp.concatenate([a_ref[...], b_ref[...]], axis=-1)
    o_ref[...] = x_ref[...] + jnp.dot(mix, w_ref[...], preferred_element_type=F32)


def _outproj2(x, a, b, w, tm=512):
    n, d = x.shape
    row = lambda wd: pl.BlockSpec((tm, wd), lambda i: (i, 0))
    return pl.pallas_call(
        _outproj2_kernel,
        grid=(n // tm,),
        in_specs=[row(d), row(a.shape[1]), row(b.shape[1]),
                  pl.BlockSpec(w.shape, lambda i: (0, 0))],
        out_specs=row(d),
        out_shape=jax.ShapeDtypeStruct((n, d), F32),
        compiler_params=_cparams(("parallel",)),
        name="outproj2",
    )(x, a, b, w)


def _outproj_merge_kernel(x_ref, o1_ref, o4_ref, o16_ref, l1_ref, l4_ref, l16_ref, e_ref, w_ref,
                          o_ref, sl4_ref, sl16_ref, so4_ref, so16_ref):
    tm = x_ref.shape[1]
    for r in range(4):
        sl4_ref[pl.ds(r, tm // 4, stride=4), :] = l4_ref[0, r]
    for r in range(16):
        sl16_ref[pl.ds(r, tm // 16, stride=16), :] = l16_ref[0, r]
    ncb = C_Q // LANES
    for cb in range(ncb):
        cs = slice(cb * LANES, (cb + 1) * LANES)
        for r in range(4):
            so4_ref[cb, pl.ds(r, tm // 4, stride=4), :] = o4_ref[0, r, :, cs].astype(F32)
        for r in range(16):
            so16_ref[cb, pl.ds(r, tm // 16, stride=16), :] = o16_ref[0, r, :, cs].astype(F32)
    lses = [l1_ref[0], sl4_ref[...], sl16_ref[...]]
    m = jnp.maximum(jnp.maximum(lses[0], lses[1]), lses[2])
    es = [jnp.exp(l - m) for l in lses]
    den = es[0] + es[1] + es[2]
    e = e_ref[...]
    w_exp = []
    for ex in es:
        wgt = ex / den
        w_hi = wgt.astype(BF16)
        w_lo = (wgt - w_hi.astype(F32)).astype(BF16)
        w_exp.append(jnp.dot(w_hi, e, preferred_element_type=F32)
                     + jnp.dot(w_lo, e, preferred_element_type=F32))
    mix = []
    for cb in range(ncb):
        cs = slice(cb * LANES, (cb + 1) * LANES)
        mix.append((w_exp[0][:, cs] * o1_ref[0, :, cs].astype(F32)
                    + w_exp[1][:, cs] * so4_ref[cb]
                    + w_exp[2][:, cs] * so16_ref[cb]).astype(BF16))
    o_ref[0] = x_ref[0] + jnp.dot(jnp.concatenate(mix, axis=-1), w_ref[...],
                                  preferred_element_type=F32)


def _outproj_merge(x, outs, lses, expand, w, tm=512):
    bsz, seq, d = x.shape
    tile = lambda wd: pl.BlockSpec((1, tm, wd), lambda b, j: (b, j, 0))
    dil = lambda dd, wd: pl.BlockSpec((1, dd, tm // dd, wd), lambda b, j: (b, 0, j, 0))
    return pl.pallas_call(
        _outproj_merge_kernel,
        grid=(bsz, seq // tm),
        in_specs=[tile(d), tile(C_Q), dil(4, C_Q), dil(16, C_Q),
                  tile(LANES), dil(4, LANES), dil(16, LANES),
                  pl.BlockSpec(expand.shape, lambda b, j: (0, 0)),
                  pl.BlockSpec(w.shape, lambda b, j: (0, 0))],
        out_specs=tile(d),
        out_shape=jax.ShapeDtypeStruct((bsz, seq, d), F32),
        scratch_shapes=[pltpu.VMEM((tm, LANES), F32), pltpu.VMEM((tm, LANES), F32),
                        pltpu.VMEM((C_Q // LANES, tm, LANES), F32),
                        pltpu.VMEM((C_Q // LANES, tm, LANES), F32)],
        compiler_params=_cparams(("parallel", "parallel")),
        name="outproj_merge",
    )(x, *outs, *lses, expand, w)


def _ffn_kernel(x_ref, g_ref, wgu_ref, wd_ref, fg_ref, o_ref, acc_ref, *, nchunk, final):
    x = x_ref[...]
    h = _rms(x, g_ref[...]).astype(BF16)
    acc_ref[...] = jnp.zeros_like(acc_ref)

    def body(c, carry):
        gate = jnp.dot(h, wgu_ref[c], preferred_element_type=F32)
        up = jnp.dot(h, wgu_ref[nchunk + c], preferred_element_type=F32)
        a = (jax.nn.silu(gate) * up).astype(BF16)
        acc_ref[...] += jnp.dot(a, wd_ref[c], preferred_element_type=F32)
        return carry

    lax.fori_loop(0, nchunk, body, 0)
    y = x + acc_ref[...]
    if final:
        y = _rms(y, fg_ref[...])
    o_ref[...] = y


def _ffn(x, g, wgu3, wd3, fg, final, tm=512):
    n, d = x.shape
    nchunk = wd3.shape[0]
    kern = functools.partial(_ffn_kernel, nchunk=nchunk, final=final)
    return pl.pallas_call(
        kern,
        grid=(n // tm,),
        in_specs=[
            pl.BlockSpec((tm, d), lambda i: (i, 0)),
            pl.BlockSpec((1, d), lambda i: (0, 0)),
            pl.BlockSpec(wgu3.shape, lambda i: (0, 0, 0)),
            pl.BlockSpec(wd3.shape, lambda i: (0, 0, 0)),
            pl.BlockSpec((1, d), lambda i: (0, 0)),
        ],
        out_specs=pl.BlockSpec((tm, d), lambda i: (i, 0)),
        out_shape=jax.ShapeDtypeStruct((n, d), F32),
        scratch_shapes=[pltpu.VMEM((tm, d), F32)],
        compiler_params=_cparams(("parallel",)),
        name="ffn",
    )(x, g.reshape(1, d), wgu3, wd3, fg.reshape(1, d))


def _tables(seq):
    inv = 1.0 / (ROPE_THETA ** (jnp.arange(0, HEAD_DIM, 2, dtype=F32) / HEAD_DIM))
    ang = jnp.arange(seq, dtype=F32)[:, None] * inv[None, :]
    ang = jnp.concatenate([ang, ang], axis=-1)
    cos, sin = jnp.cos(ang), jnp.sin(ang)
    first = (jnp.arange(HEAD_DIM) < HEAD_DIM // 2)[None, :]
    sin_lo = jnp.where(first, -sin, 0.0)
    sin_hi = jnp.where(first, 0.0, sin)
    tile2 = lambda t: jnp.concatenate([t, t], axis=-1)
    onehot = (jnp.arange(seq)[:, None] // SEL_BLOCK == jnp.arange(LANES)[None, :]).astype(F32)
    return jnp.concatenate([tile2(cos), tile2(sin_lo), tile2(sin_hi), onehot], axis=-1).astype(F32)


def _cmp_to_sel(ncp, nsel):
    c_start = np.arange(ncp) * CMP_STRIDE
    s_start = np.arange(LANES) * SEL_BLOCK
    overlap = ((c_start[:, None] <= s_start[None, :] + SEL_BLOCK - 1)
               & (c_start[:, None] + CMP_BLOCK - 1 >= s_start[None, :]))
    overlap &= (np.arange(ncp) < ncp - 1)[:, None] & (np.arange(LANES) < nsel)[None, :]
    return jnp.asarray(overlap.astype(np.float32), dtype=BF16)


def _cmp_weights(pe_k, pe_v, k_w1, v_w1, k_w2, v_w2):
    pe2 = jnp.stack([jnp.concatenate([pe, pe], axis=-1) for pe in (pe_k, pe_v)])
    w1 = jnp.stack([k_w1, v_w1]).astype(BF16).reshape(2, CMP_BLOCK, HEAD_DIM, CMP_HIDDEN)
    zero = jnp.zeros_like(w1)
    w1bd = jnp.concatenate([jnp.concatenate([w1, zero], axis=-1),
                            jnp.concatenate([zero, w1], axis=-1)], axis=-2)
    w2 = jnp.stack([k_w2, v_w2]).astype(BF16)
    return pe2, w1bd, w2


def _ffn_weights(w_gate_up, w_down, th=256):
    d = w_gate_up.shape[0]
    nchunk = FFN_HIDDEN // th
    wgu3 = w_gate_up.astype(BF16).reshape(d, 2 * nchunk, th).transpose(1, 0, 2)
    wd3 = w_down.astype(BF16).reshape(nchunk, th, d)
    return wgu3, wd3


def kernel(x, attn_norm, ffn_norm, final_norm, w_in_e, w_out_e, cmp_pe_k, cmp_pe_v,
           cmp_k_w1, cmp_k_w2, cmp_v_w1, cmp_v_w2, sinks, w_qkv_o, w_out_o,
           w_gate_up, w_down):
    bsz, seq, d = x.shape
    n = bsz * seq
    tab = _tables(seq)

    w = w_in_e[0]
    o1 = A_Q
    o2 = o1 + 6 * A_KV
    o3 = o2 + A_GATES
    o4 = o3 + B_Q
    kv = lambda j: w[:, o1 + j * A_KV:o1 + (j + 1) * A_KV]
    gate_w = jnp.pad(w[:, o2:o3], ((0, 0), (0, LANES - A_GATES)))
    w0 = jnp.concatenate(
        [w[:, :o1], kv(2), kv(4), w[:, o3:o4], w[:, o4:o4 + B_KV],
         kv(0), kv(1), kv(3), kv(5), w[:, o4 + B_KV:], gate_w], axis=1).astype(BF16)
    (qa_p, qa_r, ksa, kwd, qb_r, kbd, kcn, vcn, vsa, vwa, vbd, gates) = _proj0(
        x, attn_norm[0], w0, tab)

    pe2, w1bd, w2 = _cmp_weights(cmp_pe_k[0], cmp_pe_v[0], cmp_k_w1[0], cmp_v_w1[0],
                                 cmp_k_w2[0], cmp_v_w2[0])
    kcvc = _cmp_mlp(kcn, vcn, pe2, w1bd, w2)

    a_out = _nsa(qa_p, qa_r, gates, kcvc, _cmp_to_sel(seq // CMP_STRIDE, seq // SEL_BLOCK),
                 ksa, vsa, kwd, vwa)

    sink_row = jnp.pad(sinks[0].astype(F32), (0, LANES - B_HEADS)).reshape(1, LANES)
    r4 = lambda t: t.reshape(bsz, 1, seq, t.shape[-1])
    (b_out,) = _banded(r4(qb_r), r4(kbd), r4(vbd), pairs_per_kv=B_GROUP // 2,
                       reach=B_WINDOW - 1, nsub=2, sink=sink_row)

    xf = _outproj2(x.reshape(n, d), a_out.reshape(n, A_Q), b_out.reshape(n, B_Q),
                   w_out_e[0].astype(BF16))
    wgu3, wd3 = _ffn_weights(w_gate_up[0], w_down[0])
    xf = _ffn(xf, ffn_norm[0], wgu3, wd3, final_norm, final=False)

    x1 = xf.reshape(bsz, seq, d)
    (q_n, q_4, q_16, k_n, k_4, k_16, v_n, v_4, v_16) = _proj1(
        x1, attn_norm[1], w_qkv_o[0].astype(BF16), tab)
    pat_o, pat_l = [], []
    for (window, dil), (qq, kk, vv) in zip(
            C_PATTERNS, ((r4(q_n), r4(k_n), r4(v_n)), (q_4, k_4, v_4), (q_16, k_16, v_16))):
        o_p, l_p = _banded(qq, kk, vv, pairs_per_kv=1, reach=window // dil,
                           nsub=2, emit_lse=True)
        pat_o.append(o_p)
        pat_l.append(l_p)
    pat_o[0] = pat_o[0].reshape(bsz, seq, C_Q)
    pat_l[0] = pat_l[0].reshape(bsz, seq, LANES)
    expand = (jnp.arange(LANES)[:, None] == jnp.arange(C_Q)[None, :] // HEAD_DIM).astype(BF16)
    x2 = _outproj_merge(x1, pat_o, pat_l, expand, w_out_o[0].astype(BF16))
    wgu3, wd3 = _ffn_weights(w_gate_up[1], w_down[1])
    xf = _ffn(x2.reshape(n, d), ffn_norm[1], wgu3, wd3, final_norm, final=True)
    return xf.reshape(bsz, seq, d)
```

```python
import functools

import numpy as np
import jax
import jax.numpy as jnp
from jax import lax
from jax.experimental import pallas as pl
from jax.experimental.pallas import tpu as pltpu

F32 = jnp.float32
BF16 = jnp.bfloat16

D_MODEL = 1024
HEAD_DIM = 64
SCALE = HEAD_DIM ** -0.5
ROPE_THETA = 10000.0
NORM_EPS = 1e-5
LANES = 128
SUBLANES = 8

A_HEADS = 8
A_KV_HEADS = 2
A_GROUP = A_HEADS // A_KV_HEADS
CMP_BLOCK = 32
CMP_STRIDE = 16
CMP_HIDDEN = 256
SEL_BLOCK = 64
SEL_TOPK = 16
NSA_WINDOW = 512
N_BRANCH = 3
B_HEADS = 8
B_KV_HEADS = 2
B_GROUP = B_HEADS // B_KV_HEADS
B_WINDOW = 128
C_HEADS = 16
C_PATTERNS = ((128, 1), (512, 4), (2048, 16))
A_Q = A_HEADS * HEAD_DIM
A_KV = A_KV_HEADS * HEAD_DIM
A_GATES = A_HEADS * N_BRANCH
B_Q = B_HEADS * HEAD_DIM
B_KV = B_KV_HEADS * HEAD_DIM
C_Q = C_HEADS * HEAD_DIM
FFN_HIDDEN = 2816

NEG_BIG = -1e30
VMEM_LIMIT = 56 * 1024 * 1024

_NT = (((1,), (1,)), ((), ()))


def _cparams(sem):
    return pltpu.CompilerParams(dimension_semantics=sem, vmem_limit_bytes=VMEM_LIMIT)


def _rms(x, g):
    ms = jnp.mean(x * x, axis=-1, keepdims=True)
    return x * lax.rsqrt(ms + NORM_EPS) * g


def _low_half(rows):
    return lax.broadcasted_iota(jnp.int32, (rows, LANES), 1) < HEAD_DIM


def _rope(zc, tab_ref):
    cos = tab_ref[:, 0:LANES]
    sin_lo = tab_ref[:, LANES:2 * LANES]
    sin_hi = tab_ref[:, 2 * LANES:3 * LANES]
    return zc * cos + pltpu.roll(zc, LANES - 32, 1) * sin_lo + pltpu.roll(zc, 32, 1) * sin_hi


def _masked_pairs(pair, lo):
    zero = jnp.zeros_like(pair)
    return jnp.where(lo, pair, zero), jnp.where(lo, zero, pair)


def _proj0_kernel(x_ref, g_ref, w_ref, tab_ref, qp_ref, qr_ref, ksa_ref, kwd_ref, qb_ref,
                  kbd_ref, kcn_ref, vcn_ref, vsa_ref, vwa_ref, vbd_ref, gate_ref):
    tm = x_ref.shape[1]
    h = _rms(x_ref[0], g_ref[...]).astype(BF16)
    lo = _low_half(tm)
    onehot = tab_ref[:, 3 * LANES:4 * LANES]

    def mm(c0, width):
        return jnp.dot(h, w_ref[:, c0:c0 + width], preferred_element_type=F32)

    def cols(z, j):
        return z[:, j * LANES:(j + 1) * LANES]

    def dup(v):
        sw = pltpu.roll(v, HEAD_DIM, 1)
        return jnp.where(lo, v, sw), jnp.where(lo, sw, v)

    def aug(v):
        sw = pltpu.roll(v, HEAD_DIM, 1)
        return jnp.where(lo, v, 1.0), jnp.where(lo, sw, 1.0)

    def put(ref, j, v):
        ref[0, :, j * LANES:(j + 1) * LANES] = v.astype(ref.dtype)

    z = mm(0, A_Q)
    for j in range(A_Q // LANES):
        zc = cols(z, j)
        put(qp_ref, j, zc * SCALE)
        put(qr_ref, j, _rope(zc, tab_ref) * SCALE)
    z = mm(A_Q, 2 * LANES)
    d0, d1 = dup(_rope(cols(z, 0), tab_ref))
    put(ksa_ref, 0, d0)
    put(ksa_ref, 1, onehot)
    put(ksa_ref, 2, d1)
    put(ksa_ref, 3, onehot)
    d0, d1 = dup(_rope(cols(z, 1), tab_ref))
    put(kwd_ref, 0, d0)
    put(kwd_ref, 1, d1)
    z = mm(A_Q + 2 * LANES, B_Q)
    for j in range(B_Q // LANES):
        put(qb_ref, j, _rope(cols(z, j), tab_ref) * SCALE)
    z = mm(A_Q + 2 * LANES + B_Q, 7 * LANES)
    d0, d1 = dup(_rope(cols(z, 0), tab_ref))
    put(kbd_ref, 0, d0)
    put(kbd_ref, 1, d1)
    put(kcn_ref, 0, cols(z, 1))
    put(vcn_ref, 0, cols(z, 2))
    a0, a1 = aug(cols(z, 3))
    put(vsa_ref, 0, a0)
    put(vsa_ref, 1, a1)
    a0, a1 = aug(cols(z, 4))
    put(vwa_ref, 0, a0)
    put(vwa_ref, 1, a1)
    d0, d1 = dup(cols(z, 5))
    put(vbd_ref, 0, d0)
    put(vbd_ref, 1, d1)
    put(gate_ref, 0, jax.nn.sigmoid(cols(z, 6)))


def _proj0(x, g, w, tab, tm=512):
    bsz, seq, d = x.shape
    outs = ((A_Q, BF16), (A_Q, BF16), (4 * LANES, BF16), (2 * LANES, BF16), (B_Q, BF16),
            (2 * LANES, BF16), (LANES, F32), (LANES, F32), (2 * LANES, BF16), (2 * LANES, BF16),
            (2 * LANES, BF16), (LANES, F32))
    tile = lambda wd: pl.BlockSpec((1, tm, wd), lambda b, j: (b, j, 0))
    return pl.pallas_call(
        _proj0_kernel,
        grid=(bsz, seq // tm),
        in_specs=[tile(d),
                  pl.BlockSpec((1, d), lambda b, j: (0, 0)),
                  pl.BlockSpec(w.shape, lambda b, j: (0, 0)),
                  pl.BlockSpec((tm, 4 * LANES), lambda b, j: (j, 0))],
        out_specs=[tile(wd) for wd, _ in outs],
        out_shape=[jax.ShapeDtypeStruct((bsz, seq, wd), dt) for wd, dt in outs],
        compiler_params=_cparams(("parallel", "parallel")),
        name="proj0",
    )(x, g.reshape(1, d), w, tab)


def _proj1_kernel(x_ref, g_ref, w_ref, tab_ref, *refs):
    out_refs = refs[:9]
    sc_ref = refs[9]
    tm = x_ref.shape[1]
    h = _rms(x_ref[0], g_ref[...]).astype(BF16)
    ncb = C_Q // LANES
    for seg in range(3):
        nat_ref, d4_ref, d16_ref = out_refs[3 * seg:3 * seg + 3]
        z = jnp.dot(h, w_ref[:, seg * C_Q:(seg + 1) * C_Q], preferred_element_type=F32)
        for cb in range(ncb):
            cs = slice(cb * LANES, (cb + 1) * LANES)
            zc = z[:, cs]
            if seg < 2:
                zc = _rope(zc, tab_ref)
            if seg == 0:
                zc = zc * SCALE
            nat_ref[0, :, cs] = zc.astype(BF16)
            sc_ref[cb] = zc
        for cb in range(ncb):
            cs = slice(cb * LANES, (cb + 1) * LANES)
            for r in range(4):
                d4_ref[0, r, :, cs] = sc_ref[cb, pl.ds(r, tm // 4, stride=4), :].astype(BF16)
            for r in range(16):
                d16_ref[0, r, :, cs] = sc_ref[cb, pl.ds(r, tm // 16, stride=16), :].astype(BF16)


def _proj1(x, g, w, tab, tm=512):
    bsz, seq, d = x.shape
    tile = lambda wd: pl.BlockSpec((1, tm, wd), lambda b, j: (b, j, 0))
    dil = lambda dd: pl.BlockSpec((1, dd, tm // dd, C_Q), lambda b, j: (b, 0, j, 0))
    out_specs, out_shape = [], []
    for _ in range(3):
        out_specs += [tile(C_Q), dil(4), dil(16)]
        out_shape += [jax.ShapeDtypeStruct((bsz, seq, C_Q), BF16),
                      jax.ShapeDtypeStruct((bsz, 4, seq // 4, C_Q), BF16),
                      jax.ShapeDtypeStruct((bsz, 16, seq // 16, C_Q), BF16)]
    return pl.pallas_call(
        _proj1_kernel,
        grid=(bsz, seq // tm),
        in_specs=[tile(d),
                  pl.BlockSpec((1, d), lambda b, j: (0, 0)),
                  pl.BlockSpec(w.shape, lambda b, j: (0, 0)),
                  pl.BlockSpec((tm, 4 * LANES), lambda b, j: (j, 0))],
        out_specs=out_specs,
        out_shape=out_shape,
        scratch_shapes=[pltpu.VMEM((C_Q // LANES, tm, LANES), F32)],
        compiler_params=_cparams(("parallel", "parallel")),
        name="proj1",
    )(x, g.reshape(1, d), w, tab)


def _cmp_mlp_kernel(kc_ref, vc_ref, pe_ref, w1_ref, w2_ref, o_ref):
    nc = o_ref.shape[2]
    for s, src in enumerate((kc_ref, vc_ref)):
        hid_a = jnp.zeros((nc, 2 * CMP_HIDDEN), F32)
        hid_b = jnp.zeros((nc, 2 * CMP_HIDDEN), F32)
        for p in range(CMP_STRIDE):
            xp = src[0, pl.ds(p, nc, stride=CMP_STRIDE), :]
            xa = (xp + pe_ref[s, p:p + 1, :]).astype(BF16)
            xb = (xp + pe_ref[s, CMP_STRIDE + p:CMP_STRIDE + p + 1, :]).astype(BF16)
            hid_a = hid_a + jnp.dot(xa, w1_ref[s, p], preferred_element_type=F32)
            hid_b = hid_b + jnp.dot(xb, w1_ref[s, CMP_STRIDE + p], preferred_element_type=F32)
        hid = hid_a + pltpu.roll(hid_b, nc - 1, 0)
        a = jax.nn.silu(hid).astype(BF16)
        for hh in range(A_KV_HEADS):
            o_ref[0, 2 * s + hh] = jnp.dot(a[:, hh * CMP_HIDDEN:(hh + 1) * CMP_HIDDEN], w2_ref[s],
                                           preferred_element_type=F32)


def _cmp_mlp(kcn, vcn, pe2, w1bd, w2):
    bsz, seq, _ = kcn.shape
    nc = seq // CMP_STRIDE
    nat = pl.BlockSpec((1, seq, LANES), lambda b: (b, 0, 0))
    return pl.pallas_call(
        _cmp_mlp_kernel,
        grid=(bsz,),
        in_specs=[nat, nat,
                  pl.BlockSpec(pe2.shape, lambda b: (0, 0, 0)),
                  pl.BlockSpec(w1bd.shape, lambda b: (0, 0, 0, 0)),
                  pl.BlockSpec(w2.shape, lambda b: (0, 0, 0))],
        out_specs=pl.BlockSpec((1, 2 * A_KV_HEADS, nc, HEAD_DIM), lambda b: (b, 0, 0, 0)),
        out_shape=jax.ShapeDtypeStruct((bsz, 2 * A_KV_HEADS, nc, HEAD_DIM), F32),
        compiler_params=_cparams(("parallel",)),
        name="cmp_mlp",
    )(kcn, vcn, pe2, w1bd, w2)


def _nsa_kernel(qp_ref, qr_ref, gate_ref, kcvc_ref, c2s_ref, ksa_ref, vsa_ref,
                kwd_ref, vwa_ref, o_ref, *, tq, tk):
    i = pl.program_id(1)
    q0 = i * tq
    ncp = kcvc_ref.shape[2]
    nsel = ksa_ref.shape[1] // SEL_BLOCK
    G = A_GROUP
    NH = A_KV_HEADS
    rows = G * tq
    lo = _low_half(tq)
    streams = range(NH)

    def masked_heads(ref, kvh):
        out = []
        for pr in range(G // 2):
            c0 = (kvh * (G // 2) + pr) * LANES
            out.extend(_masked_pairs(ref[0, :, c0:c0 + LANES], lo))
        return out

    tpos_c = q0 + lax.broadcasted_iota(jnp.int32, (tq, ncp), 0)
    cidx = lax.broadcasted_iota(jnp.int32, (tq, ncp), 1)
    cvalid = (cidx * CMP_STRIDE + (CMP_BLOCK - 1) <= tpos_c) & (cidx < ncp - 1)
    cbias = jnp.where(cvalid, 0.0, -jnp.inf).astype(F32)
    cbias4 = jnp.concatenate([cbias] * G, axis=0)
    o_cmp, psum = [], []
    for kvh in streams:
        kc = kcvc_ref[0, kvh].astype(BF16)
        vc = kcvc_ref[0, NH + kvh].astype(BF16)
        kc2 = jnp.concatenate([kc, kc], axis=1)
        vc2 = jnp.concatenate([vc, vc], axis=1)
        qc = jnp.concatenate(masked_heads(qp_ref, kvh), axis=0)
        s = lax.dot_general(qc, kc2, _NT, preferred_element_type=F32) + cbias4
        m = jnp.max(s, axis=-1, keepdims=True)
        m = jnp.where(m == -jnp.inf, 0.0, m)
        p = jnp.exp(s - m)
        den = jnp.sum(p, axis=-1, keepdims=True)
        p = p / jnp.where(den > 0, den, 1.0)
        o_cmp.append(jnp.dot(p.astype(BF16), vc2, preferred_element_type=F32))
        ps = p[0:tq]
        for g in range(1, G):
            ps = ps + p[g * tq:(g + 1) * tq]
        psum.append(ps)

    nkw = NSA_WINDOW + tq
    start = pl.multiple_of(jnp.maximum(q0 - NSA_WINDOW, 0), tq)
    rq_w = q0 + lax.broadcasted_iota(jnp.int32, (tq, nkw), 0)
    kpos = start + lax.broadcasted_iota(jnp.int32, (tq, nkw), 1)
    diff = rq_w - kpos
    wbias = jnp.where((diff >= 0) & (diff < NSA_WINDOW), 0.0, -jnp.inf).astype(F32)
    wbias4 = jnp.concatenate([wbias] * G, axis=0)
    q_heads = [masked_heads(qr_ref, kvh) for kvh in streams]
    acc_win = []
    for kvh in streams:
        kw = kwd_ref[0, pl.ds(start, nkw), kvh * LANES:(kvh + 1) * LANES]
        vw = vwa_ref[0, pl.ds(start, nkw), kvh * LANES:(kvh + 1) * LANES]
        qs = jnp.concatenate(q_heads[kvh], axis=0)
        s = lax.dot_general(qs, kw, _NT, preferred_element_type=F32) + wbias4
        m = jnp.max(s, axis=-1, keepdims=True)
        p = jnp.exp(s - m)
        acc_win.append(jnp.dot(p.astype(BF16), vw, preferred_element_type=F32))

    c2s = c2s_ref[...]
    blk = lax.broadcasted_iota(jnp.int32, (LANES, tq), 0)
    cur = jnp.right_shift(q0 + lax.broadcasted_iota(jnp.int32, (LANES, tq), 1),
                          SEL_BLOCK.bit_length() - 1)
    forced = (blk == 0) | (blk == cur) | (blk == cur - 1)
    future = blk > cur
    blk_g = lax.broadcasted_iota(jnp.int32, (SUBLANES, tq), 0)
    ngrp = nsel // SUBLANES
    qs_aug = []
    for kvh in streams:
        p_hi = psum[kvh].astype(BF16)
        p_lo = (psum[kvh] - p_hi.astype(F32)).astype(BF16)
        imp = (jnp.dot(p_hi, c2s, preferred_element_type=F32)
               + jnp.dot(p_lo, c2s, preferred_element_type=F32))
        imp_t = jnp.where(forced, jnp.inf, jnp.where(future, -jnp.inf, imp.T))
        imp_g = [imp_t[r * SUBLANES:(r + 1) * SUBLANES] for r in range(ngrp)]
        cnt_g = [jnp.zeros((SUBLANES, tq), jnp.int32) for _ in range(ngrp)]
        for mblk in range(nsel):
            row = imp_t[mblk:mblk + 1, :]
            for r in range(ngrp):
                first = r * SUBLANES
                if first > mblk:
                    inc = (row >= imp_g[r]).astype(jnp.int32)
                elif first + SUBLANES - 1 <= mblk:
                    inc = (row > imp_g[r]).astype(jnp.int32)
                else:
                    inc = jnp.where(blk_g + first > mblk, (row >= imp_g[r]).astype(jnp.int32),
                                    (row > imp_g[r]).astype(jnp.int32))
                cnt_g[r] = cnt_g[r] + inc
        bias_t = jnp.concatenate(
            [jnp.where(c < SEL_TOPK, 0.0, NEG_BIG).astype(F32) for c in cnt_g]
            + [jnp.zeros((LANES - nsel, tq), F32)], axis=0)
        bias = bias_t.T.astype(BF16)
        qs_aug.append(jnp.concatenate(
            [jnp.concatenate([qh, bias], axis=1) for qh in q_heads[kvh]], axis=0))

    def sel_chunk(c, carry, diagonal):
        off = pl.multiple_of(c * tk, tk)
        out = []
        for kvh in streams:
            m, acc = carry[kvh]
            k = ksa_ref[0, pl.ds(off, tk), kvh * 2 * LANES:(kvh + 1) * 2 * LANES]
            v = vsa_ref[0, pl.ds(off, tk), kvh * LANES:(kvh + 1) * LANES]
            s = lax.dot_general(qs_aug[kvh], k, _NT, preferred_element_type=F32)
            if diagonal:
                rq = q0 + (lax.broadcasted_iota(jnp.int32, (rows, tk), 0) & (tq - 1))
                ck = off + lax.broadcasted_iota(jnp.int32, (rows, tk), 1)
                s = jnp.where(ck <= rq, s, NEG_BIG)
            m_new = jnp.maximum(m, jnp.max(s, axis=-1, keepdims=True))
            alpha = jnp.exp(m - m_new)
            p = jnp.exp(s - m_new)
            acc = alpha * acc + jnp.dot(p.astype(BF16), v, preferred_element_type=F32)
            out.append((m_new, acc))
        return tuple(out)

    n_full = q0 // tk
    init = tuple((jnp.full((rows, 1), -jnp.inf, F32), jnp.zeros((rows, LANES), F32)) for _ in streams)
    carry = lax.fori_loop(0, n_full, lambda c, cr: sel_chunk(c, cr, False), init)
    carry = sel_chunk(n_full, carry, True)

    gates = gate_ref[0]
    for kvh in streams:
        acc_sel = carry[kvh][1]
        sel_sw = pltpu.roll(acc_sel, HEAD_DIM, 1)
        win_sw = pltpu.roll(acc_win[kvh], HEAD_DIM, 1)
        for pr in range(G // 2):
            halves = []
            for hh in range(2):
                g = 2 * pr + hh
                rs = slice(g * tq, (g + 1) * tq)
                if hh == 0:
                    o_s = acc_sel[rs] / sel_sw[rs]
                    o_w = acc_win[kvh][rs] / win_sw[rs]
                else:
                    o_s = sel_sw[rs] / acc_sel[rs]
                    o_w = win_sw[rs] / acc_win[kvh][rs]
                gc = (kvh * G + g) * N_BRANCH
                halves.append(gates[:, gc:gc + 1] * o_cmp[kvh][rs] + gates[:, gc + 1:gc + 2] * o_s
                              + gates[:, gc + 2:gc + 3] * o_w)
            c0 = (kvh * (G // 2) + pr) * LANES
            o_ref[0, :, c0:c0 + LANES] = jnp.where(lo, halves[0], halves[1]).astype(o_ref.dtype)


def _nsa(qp, qr, gates, kcvc, c2s, ksa, vsa, kwd, vwa, tq=256, tk=1024):
    bsz, seq, _ = qp.shape
    ncp = kcvc.shape[2]
    kern = functools.partial(_nsa_kernel, tq=tq, tk=tk)
    qtile = pl.BlockSpec((1, tq, A_Q), lambda b, i: (b, i, 0))
    res = lambda wd: pl.BlockSpec((1, seq, wd), lambda b, i: (b, 0, 0))
    return pl.pallas_call(
        kern,
        grid=(bsz, seq // tq),
        in_specs=[
            qtile, qtile,
            pl.BlockSpec((1, tq, LANES), lambda b, i: (b, i, 0)),
            pl.BlockSpec((1, 2 * A_KV_HEADS, ncp, HEAD_DIM), lambda b, i: (b, 0, 0, 0)),
            pl.BlockSpec((ncp, LANES), lambda b, i: (0, 0)),
            res(4 * LANES), res(2 * LANES), res(2 * LANES), res(2 * LANES),
        ],
        out_specs=qtile,
        out_shape=jax.ShapeDtypeStruct((bsz, seq, A_Q), BF16),
        compiler_params=_cparams(("parallel", "arbitrary")),
        name="nsa",
    )(qp, qr, gates, kcvc, c2s, ksa, vsa, kwd, vwa)


def _banded_kernel(*refs, npairs, pairs_per_kv, reach, tq, nsub, has_sink, emit_lse):
    q_ref, kp_ref, kc_ref, vp_ref, vc_ref = refs[:5]
    pos = 5
    sink_ref = None
    if has_sink:
        sink_ref = refs[pos]
        pos += 1
    o_ref = refs[pos]
    lse_ref = refs[pos + 1] if emit_lse else None

    i = pl.program_id(2)
    nk = 2 * tq
    lo = _low_half(tq)
    lane = lax.broadcasted_iota(jnp.int32, (tq, LANES), 1)
    row = lax.broadcasted_iota(jnp.int32, (tq, nk), 0)
    col = lax.broadcasted_iota(jnp.int32, (tq, nk), 1)
    diff = tq + row - col
    band = (diff >= 0) & (diff <= reach)
    bias_band = jnp.where(band, 0.0, -jnp.inf).astype(F32)
    bias_first = jnp.where(band & ((col >= tq) | (i > 0)), 0.0, -jnp.inf).astype(F32)
    bias2 = [jnp.concatenate([b, b], axis=0) for b in (bias_first, bias_band)]

    nkv = npairs // pairs_per_kv
    k_all = [jnp.concatenate([kp_ref[0, 0, :, c * LANES:(c + 1) * LANES],
                              kc_ref[0, 0, :, c * LANES:(c + 1) * LANES]], axis=0) for c in range(nkv)]
    v_all = [jnp.concatenate([vp_ref[0, 0, :, c * LANES:(c + 1) * LANES],
                              vc_ref[0, 0, :, c * LANES:(c + 1) * LANES]], axis=0) for c in range(nkv)]

    parts = []
    for j in range(nsub):
        for p in range(npairs):
            qpair = q_ref[0, 0, j * tq:(j + 1) * tq, p * LANES:(p + 1) * LANES]
            q2 = jnp.concatenate(_masked_pairs(qpair, lo), axis=0)
            kk = k_all[p // pairs_per_kv][j * tq:(j + 2) * tq]
            sp = lax.dot_general(q2, kk, _NT, preferred_element_type=F32)
            parts.append(sp + bias2[min(j, 1)])
    s = jnp.concatenate(parts, axis=0)
    m = jnp.max(s, axis=-1, keepdims=True)
    if has_sink:
        sink = jnp.concatenate(
            [jnp.broadcast_to(sink_ref[0:1, 2 * p + hh:2 * p + hh + 1], (tq, 1))
             for j in range(nsub) for p in range(npairs) for hh in range(2)], axis=0)
        m = jnp.maximum(m, sink)
    pexp = jnp.exp(s - m)
    den = jnp.sum(pexp, axis=-1, keepdims=True)
    if has_sink:
        den = den + jnp.exp(sink - m)
    pb = pexp.astype(BF16)
    if emit_lse:
        lse = m + jnp.log(den)
    for j in range(nsub):
        stats = jnp.zeros((tq, LANES), F32)
        for p in range(npairs):
            part = j * npairs + p
            rs = slice(part * 2 * tq, (part + 1) * 2 * tq)
            vv = v_all[p // pairs_per_kv][j * tq:(j + 2) * tq]
            r = jnp.dot(pb[rs], vv, preferred_element_type=F32) / den[rs]
            o_ref[0, 0, j * tq:(j + 1) * tq, p * LANES:(p + 1) * LANES] = (
                jnp.where(lo, r[:tq], r[tq:]).astype(o_ref.dtype))
            if emit_lse:
                for hh in range(2):
                    first = part * 2 * tq + hh * tq
                    stats = jnp.where(lane == 2 * p + hh, lse[first:first + tq], stats)
        if emit_lse:
            lse_ref[0, 0, j * tq:(j + 1) * tq, :] = stats


def _banded(q, k, v, *, pairs_per_kv, reach, nsub, sink=None, emit_lse=False, tq=128):
    bsz, nres, length, cq = q.shape
    ck = k.shape[-1]
    npairs = cq // LANES
    assert reach <= tq and length % (nsub * tq) == 0
    kern = functools.partial(_banded_kernel, npairs=npairs, pairs_per_kv=pairs_per_kv, reach=reach,
                             tq=tq, nsub=nsub, has_sink=sink is not None, emit_lse=emit_lse)
    cur = lambda wd: pl.BlockSpec((1, 1, nsub * tq, wd), lambda b, r, i: (b, r, i, 0))
    prev = lambda wd: pl.BlockSpec((1, 1, tq, wd),
                                   lambda b, r, i: (b, r, jnp.maximum(i * nsub - 1, 0), 0))
    in_specs = [cur(cq), prev(ck), cur(ck), prev(ck), cur(ck)]
    args = [q, k, k, v, v]
    if sink is not None:
        in_specs.append(pl.BlockSpec((1, LANES), lambda b, r, i: (0, 0)))
        args.append(sink)
    out_specs = [cur(cq)]
    out_shape = [jax.ShapeDtypeStruct(q.shape, BF16)]
    if emit_lse:
        out_specs.append(cur(LANES))
        out_shape.append(jax.ShapeDtypeStruct((bsz, nres, length, LANES), F32))
    return pl.pallas_call(
        kern,
        grid=(bsz, nres, length // (nsub * tq)),
        in_specs=in_specs,
        out_specs=out_specs,
        out_shape=out_shape,
        compiler_params=_cparams(("parallel", "parallel", "arbitrary")),
        name="banded",
    )(*args)


def _outproj2_kernel(x_ref, a_ref, b_ref, w_ref, o_ref):
    mix = jnp.concatenate([a_ref[...], b_ref[...]], axis=-1)
    o_ref[...] = x_ref[...] + jnp.dot(mix, w_ref[...], preferred_element_type=F32)


def _outproj2(x, a, b, w, tm=512):
    n, d = x.shape
    row = lambda wd: pl.BlockSpec((tm, wd), lambda i: (i, 0))
    return pl.pallas_call(
        _outproj2_kernel,
        grid=(n // tm,),
        in_specs=[row(d), row(a.shape[1]), row(b.shape[1]),
                  pl.BlockSpec(w.shape, lambda i: (0, 0))],
        out_specs=row(d),
        out_shape=jax.ShapeDtypeStruct((n, d), F32),
        compiler_params=_cparams(("parallel",)),
        name="outproj2",
    )(x, a, b, w)


def _outproj_merge_kernel(x_ref, o1_ref, o4_ref, o16_ref, l1_ref, l4_ref, l16_ref, e_ref, w_ref,
                          o_ref, sl4_ref, sl16_ref, so4_ref, so16_ref):
    tm = x_ref.shape[1]
    for r in range(4):
        sl4_ref[pl.ds(r, tm // 4, stride=4), :] = l4_ref[0, r]
    for r in range(16):
        sl16_ref[pl.ds(r, tm // 16, stride=16), :] = l16_ref[0, r]
    ncb = C_Q // LANES
    for cb in range(ncb):
        cs = slice(cb * LANES, (cb + 1) * LANES)
        for r in range(4):
            so4_ref[cb, pl.ds(r, tm // 4, stride=4), :] = o4_ref[0, r, :, cs].astype(F32)
        for r in range(16):
            so16_ref[cb, pl.ds(r, tm // 16, stride=16), :] = o16_ref[0, r, :, cs].astype(F32)
    lses = [l1_ref[0], sl4_ref[...], sl16_ref[...]]
    m = jnp.maximum(jnp.maximum(lses[0], lses[1]), lses[2])
    es = [jnp.exp(l - m) for l in lses]
    den = es[0] + es[1] + es[2]
    e = e_ref[...]
    w_exp = []
    for ex in es:
        wgt = ex / den
        w_hi = wgt.astype(BF16)
        w_lo = (wgt - w_hi.astype(F32)).astype(BF16)
        w_exp.append(jnp.dot(w_hi, e, preferred_element_type=F32)
                     + jnp.dot(w_lo, e, preferred_element_type=F32))
    mix = []
    for cb in range(ncb):
        cs = slice(cb * LANES, (cb + 1) * LANES)
        mix.append((w_exp[0][:, cs] * o1_ref[0, :, cs].astype(F32)
                    + w_exp[1][:, cs] * so4_ref[cb]
                    + w_exp[2][:, cs] * so16_ref[cb]).astype(BF16))
    o_ref[0] = x_ref[0] + jnp.dot(jnp.concatenate(mix, axis=-1), w_ref[...],
                                  preferred_element_type=F32)


def _outproj_merge(x, outs, lses, expand, w, tm=512):
    bsz, seq, d = x.shape
    tile = lambda wd: pl.BlockSpec((1, tm, wd), lambda b, j: (b, j, 0))
    dil = lambda dd, wd: pl.BlockSpec((1, dd, tm // dd, wd), lambda b, j: (b, 0, j, 0))
    return pl.pallas_call(
        _outproj_merge_kernel,
        grid=(bsz, seq // tm),
        in_specs=[tile(d), tile(C_Q), dil(4, C_Q), dil(16, C_Q),
                  tile(LANES), dil(4, LANES), dil(16, LANES),
                  pl.BlockSpec(expand.shape, lambda b, j: (0, 0)),
                  pl.BlockSpec(w.shape, lambda b, j: (0, 0))],
        out_specs=tile(d),
        out_shape=jax.ShapeDtypeStruct((bsz, seq, d), F32),
        scratch_shapes=[pltpu.VMEM((tm, LANES), F32), pltpu.VMEM((tm, LANES), F32),
                        pltpu.VMEM((C_Q // LANES, tm, LANES), F32),
                        pltpu.VMEM((C_Q // LANES, tm, LANES), F32)],
        compiler_params=_cparams(("parallel", "parallel")),
        name="outproj_merge",
    )(x, *outs, *lses, expand, w)


def _ffn_kernel(x_ref, g_ref, wgu_ref, wd_ref, fg_ref, o_ref, acc_ref, *, nchunk, final):
    x = x_ref[...]
    h = _rms(x, g_ref[...]).astype(BF16)
    acc_ref[...] = jnp.zeros_like(acc_ref)

    def body(c, carry):
        gate = jnp.dot(h, wgu_ref[c], preferred_element_type=F32)
        up = jnp.dot(h, wgu_ref[nchunk + c], preferred_element_type=F32)
        a = (jax.nn.silu(gate) * up).astype(BF16)
        acc_ref[...] += jnp.dot(a, wd_ref[c], preferred_element_type=F32)
        return carry

    lax.fori_loop(0, nchunk, body, 0)
    y = x + acc_ref[...]
    if final:
        y = _rms(y, fg_ref[...])
    o_ref[...] = y


def _ffn(x, g, wgu3, wd3, fg, final, tm=512):
    n, d = x.shape
    nchunk = wd3.shape[0]
    kern = functools.partial(_ffn_kernel, nchunk=nchunk, final=final)
    return pl.pallas_call(
        kern,
        grid=(n // tm,),
        in_specs=[
            pl.BlockSpec((tm, d), lambda i: (i, 0)),
            pl.BlockSpec((1, d), lambda i: (0, 0)),
            pl.BlockSpec(wgu3.shape, lambda i: (0, 0, 0)),
            pl.BlockSpec(wd3.shape, lambda i: (0, 0, 0)),
            pl.BlockSpec((1, d), lambda i: (0, 0)),
        ],
        out_specs=pl.BlockSpec((tm, d), lambda i: (i, 0)),
        out_shape=jax.ShapeDtypeStruct((n, d), F32),
        scratch_shapes=[pltpu.VMEM((tm, d), F32)],
        compiler_params=_cparams(("parallel",)),
        name="ffn",
    )(x, g.reshape(1, d), wgu3, wd3, fg.reshape(1, d))


def _tables(seq):
    inv = 1.0 / (ROPE_THETA ** (jnp.arange(0, HEAD_DIM, 2, dtype=F32) / HEAD_DIM))
    ang = jnp.arange(seq, dtype=F32)[:, None] * inv[None, :]
    ang = jnp.concatenate([ang, ang], axis=-1)
    cos, sin = jnp.cos(ang), jnp.sin(ang)
    first = (jnp.arange(HEAD_DIM) < HEAD_DIM // 2)[None, :]
    sin_lo = jnp.where(first, -sin, 0.0)
    sin_hi = jnp.where(first, 0.0, sin)
    tile2 = lambda t: jnp.concatenate([t, t], axis=-1)
    onehot = (jnp.arange(seq)[:, None] // SEL_BLOCK == jnp.arange(LANES)[None, :]).astype(F32)
    return jnp.concatenate([tile2(cos), tile2(sin_lo), tile2(sin_hi), onehot], axis=-1).astype(F32)


def _cmp_to_sel(ncp, nsel):
    c_start = np.arange(ncp) * CMP_STRIDE
    s_start = np.arange(LANES) * SEL_BLOCK
    overlap = ((c_start[:, None] <= s_start[None, :] + SEL_BLOCK - 1)
               & (c_start[:, None] + CMP_BLOCK - 1 >= s_start[None, :]))
    overlap &= (np.arange(ncp) < ncp - 1)[:, None] & (np.arange(LANES) < nsel)[None, :]
    return jnp.asarray(overlap.astype(np.float32), dtype=BF16)


def _cmp_weights(pe_k, pe_v, k_w1, v_w1, k_w2, v_w2):
    pe2 = jnp.stack([jnp.concatenate([pe, pe], axis=-1) for pe in (pe_k, pe_v)])
    w1 = jnp.stack([k_w1, v_w1]).astype(BF16).reshape(2, CMP_BLOCK, HEAD_DIM, CMP_HIDDEN)
    zero = jnp.zeros_like(w1)
    w1bd = jnp.concatenate([jnp.concatenate([w1, zero], axis=-1),
                            jnp.concatenate([zero, w1], axis=-1)], axis=-2)
    w2 = jnp.stack([k_w2, v_w2]).astype(BF16)
    return pe2, w1bd, w2


def _ffn_weights(w_gate_up, w_down, th=1408):
    d = w_gate_up.shape[0]
    nchunk = FFN_HIDDEN // th
    wgu3 = w_gate_up.astype(BF16).reshape(d, 2 * nchunk, th).transpose(1, 0, 2)
    wd3 = w_down.astype(BF16).reshape(nchunk, th, d)
    return wgu3, wd3


def kernel(x, attn_norm, ffn_norm, final_norm, w_in_e, w_out_e, cmp_pe_k, cmp_pe_v,
           cmp_k_w1, cmp_k_w2, cmp_v_w1, cmp_v_w2, sinks, w_qkv_o, w_out_o,
           w_gate_up, w_down):
    bsz, seq, d = x.shape
    n = bsz * seq
    tab = _tables(seq)

    w = w_in_e[0]
    o1 = A_Q
    o2 = o1 + 6 * A_KV
    o3 = o2 + A_GATES
    o4 = o3 + B_Q
    kv = lambda j: w[:, o1 + j * A_KV:o1 + (j + 1) * A_KV]
    gate_w = jnp.pad(w[:, o2:o3], ((0, 0), (0, LANES - A_GATES)))
    w0 = jnp.concatenate(
        [w[:, :o1], kv(2), kv(4), w[:, o3:o4], w[:, o4:o4 + B_KV],
         kv(0), kv(1), kv(3), kv(5), w[:, o4 + B_KV:], gate_w], axis=1).astype(BF16)
    (qa_p, qa_r, ksa, kwd, qb_r, kbd, kcn, vcn, vsa, vwa, vbd, gates) = _proj0(
        x, attn_norm[0], w0, tab)

    pe2, w1bd, w2 = _cmp_weights(cmp_pe_k[0], cmp_pe_v[0], cmp_k_w1[0], cmp_v_w1[0],
                                 cmp_k_w2[0], cmp_v_w2[0])
    kcvc = _cmp_mlp(kcn, vcn, pe2, w1bd, w2)

    a_out = _nsa(qa_p, qa_r, gates, kcvc, _cmp_to_sel(seq // CMP_STRIDE, seq // SEL_BLOCK),
                 ksa, vsa, kwd, vwa)

    sink_row = jnp.pad(sinks[0].astype(F32), (0, LANES - B_HEADS)).reshape(1, LANES)
    r4 = lambda t: t.reshape(bsz, 1, seq, t.shape[-1])
    (b_out,) = _banded(r4(qb_r), r4(kbd), r4(vbd), pairs_per_kv=B_GROUP // 2,
                       reach=B_WINDOW - 1, nsub=4, sink=sink_row)

    xf = _outproj2(x.reshape(n, d), a_out.reshape(n, A_Q), b_out.reshape(n, B_Q),
                   w_out_e[0].astype(BF16))
    wgu3, wd3 = _ffn_weights(w_gate_up[0], w_down[0])
    xf = _ffn(xf, ffn_norm[0], wgu3, wd3, final_norm, final=False)

    x1 = xf.reshape(bsz, seq, d)
    (q_n, q_4, q_16, k_n, k_4, k_16, v_n, v_4, v_16) = _proj1(
        x1, attn_norm[1], w_qkv_o[0].astype(BF16), tab)
    pat_o, pat_l = [], []
    for (window, dil), (qq, kk, vv) in zip(
            C_PATTERNS, ((r4(q_n), r4(k_n), r4(v_n)), (q_4, k_4, v_4), (q_16, k_16, v_16))):
        o_p, l_p = _banded(qq, kk, vv, pairs_per_kv=1, reach=window // dil,
                           nsub=2, emit_lse=True)
        pat_o.append(o_p)
        pat_l.append(l_p)
    pat_o[0] = pat_o[0].reshape(bsz, seq, C_Q)
    pat_l[0] = pat_l[0].reshape(bsz, seq, LANES)
    expand = (jnp.arange(LANES)[:, None] == jnp.arange(C_Q)[None, :] // HEAD_DIM).astype(BF16)
    x2 = _outproj_merge(x1, pat_o, pat_l, expand, w_out_o[0].astype(BF16))
    wgu3, wd3 = _ffn_weights(w_gate_up[1], w_down[1])
    xf = _ffn(x2.reshape(n, d), ffn_norm[1], wgu3, wd3, final_norm, final=True)
    return xf.reshape(bsz, seq, d)
```

```python
import functools

import numpy as np
import jax
import jax.numpy as jnp
from jax import lax
from jax.experimental import pallas as pl
from jax.experimental.pallas import tpu as pltpu

F32 = jnp.float32
BF16 = jnp.bfloat16

D_MODEL = 1024
HEAD_DIM = 64
SCALE = HEAD_DIM ** -0.5
ROPE_THETA = 10000.0
NORM_EPS = 1e-5
LANES = 128
SUBLANES = 8

A_HEADS = 8
A_KV_HEADS = 2
A_GROUP = A_HEADS // A_KV_HEADS
CMP_BLOCK = 32
CMP_STRIDE = 16
CMP_HIDDEN = 256
SEL_BLOCK = 64
SEL_TOPK = 16
NSA_WINDOW = 512
N_BRANCH = 3
B_HEADS = 8
B_KV_HEADS = 2
B_GROUP = B_HEADS // B_KV_HEADS
B_WINDOW = 128
C_HEADS = 16
C_PATTERNS = ((128, 1), (512, 4), (2048, 16))
A_Q = A_HEADS * HEAD_DIM
A_KV = A_KV_HEADS * HEAD_DIM
A_GATES = A_HEADS * N_BRANCH
B_Q = B_HEADS * HEAD_DIM
B_KV = B_KV_HEADS * HEAD_DIM
C_Q = C_HEADS * HEAD_DIM
FFN_HIDDEN = 2816

LOG2E = float(np.log2(np.e))
NEG_BIG = -1e30
VMEM_LIMIT = 56 * 1024 * 1024

_NT = (((1,), (1,)), ((), ()))


def _cparams(sem):
    return pltpu.CompilerParams(dimension_semantics=sem, vmem_limit_bytes=VMEM_LIMIT)


def _rms(x, g):
    ms = jnp.mean(x * x, axis=-1, keepdims=True)
    return x * lax.rsqrt(ms + NORM_EPS) * g


def _low_half(rows):
    return lax.broadcasted_iota(jnp.int32, (rows, LANES), 1) < HEAD_DIM


def _rope(zc, tab_ref):
    cos = tab_ref[:, 0:LANES]
    sin_lo = tab_ref[:, LANES:2 * LANES]
    sin_hi = tab_ref[:, 2 * LANES:3 * LANES]
    return zc * cos + pltpu.roll(zc, LANES - 32, 1) * sin_lo + pltpu.roll(zc, 32, 1) * sin_hi


def _masked_pairs(pair, lo):
    zero = jnp.zeros_like(pair)
    return jnp.where(lo, pair, zero), jnp.where(lo, zero, pair)


def _proj0_kernel(x_ref, g_ref, w_ref, tab_ref, qp_ref, qr_ref, ksa_ref, kwd_ref, qb_ref,
                  kbd_ref, kcn_ref, vcn_ref, vsa_ref, vwa_ref, vbd_ref, gate_ref):
    tm = x_ref.shape[1]
    h = _rms(x_ref[0], g_ref[...]).astype(BF16)
    lo = _low_half(tm)
    onehot = tab_ref[:, 3 * LANES:4 * LANES]

    def mm(c0, width):
        return jnp.dot(h, w_ref[:, c0:c0 + width], preferred_element_type=F32)

    def cols(z, j):
        return z[:, j * LANES:(j + 1) * LANES]

    def dup(v):
        sw = pltpu.roll(v, HEAD_DIM, 1)
        return jnp.where(lo, v, sw), jnp.where(lo, sw, v)

    def aug(v):
        sw = pltpu.roll(v, HEAD_DIM, 1)
        return jnp.where(lo, v, 1.0), jnp.where(lo, sw, 1.0)

    def put(ref, j, v):
        ref[0, :, j * LANES:(j + 1) * LANES] = v.astype(ref.dtype)

    z = mm(0, A_Q)
    for j in range(A_Q // LANES):
        zc = cols(z, j)
        put(qp_ref, j, zc * SCALE)
        put(qr_ref, j, _rope(zc, tab_ref) * (SCALE * LOG2E))
    z = mm(A_Q, 2 * LANES)
    d0, d1 = dup(_rope(cols(z, 0), tab_ref))
    put(ksa_ref, 0, d0)
    put(ksa_ref, 1, onehot)
    put(ksa_ref, 2, d1)
    put(ksa_ref, 3, onehot)
    d0, d1 = dup(_rope(cols(z, 1), tab_ref))
    put(kwd_ref, 0, d0)
    put(kwd_ref, 1, d1)
    z = mm(A_Q + 2 * LANES, B_Q)
    for j in range(B_Q // LANES):
        put(qb_ref, j, _rope(cols(z, j), tab_ref) * SCALE)
    z = mm(A_Q + 2 * LANES + B_Q, 7 * LANES)
    d0, d1 = dup(_rope(cols(z, 0), tab_ref))
    put(kbd_ref, 0, d0)
    put(kbd_ref, 1, d1)
    put(kcn_ref, 0, cols(z, 1))
    put(vcn_ref, 0, cols(z, 2))
    a0, a1 = aug(cols(z, 3))
    put(vsa_ref, 0, a0)
    put(vsa_ref, 1, a1)
    a0, a1 = aug(cols(z, 4))
    put(vwa_ref, 0, a0)
    put(vwa_ref, 1, a1)
    d0, d1 = dup(cols(z, 5))
    put(vbd_ref, 0, d0)
    put(vbd_ref, 1, d1)
    put(gate_ref, 0, jax.nn.sigmoid(cols(z, 6)))


def _proj0(x, g, w, tab, tm=512):
    bsz, seq, d = x.shape
    outs = ((A_Q, BF16), (A_Q, BF16), (4 * LANES, BF16), (2 * LANES, BF16), (B_Q, BF16),
            (2 * LANES, BF16), (LANES, F32), (LANES, F32), (2 * LANES, BF16), (2 * LANES, BF16),
            (2 * LANES, BF16), (LANES, F32))
    tile = lambda wd: pl.BlockSpec((1, tm, wd), lambda b, j: (b, j, 0))
    return pl.pallas_call(
        _proj0_kernel,
        grid=(bsz, seq // tm),
        in_specs=[tile(d),
                  pl.BlockSpec((1, d), lambda b, j: (0, 0)),
                  pl.BlockSpec(w.shape, lambda b, j: (0, 0)),
                  pl.BlockSpec((tm, 4 * LANES), lambda b, j: (j, 0))],
        out_specs=[tile(wd) for wd, _ in outs],
        out_shape=[jax.ShapeDtypeStruct((bsz, seq, wd), dt) for wd, dt in outs],
        compiler_params=_cparams(("parallel", "parallel")),
        name="proj0",
    )(x, g.reshape(1, d), w, tab)


def _proj1_kernel(x_ref, g_ref, w_ref, tab_ref, *refs):
    out_refs = refs[:9]
    sc_ref = refs[9]
    tm = x_ref.shape[1]
    h = _rms(x_ref[0], g_ref[...]).astype(BF16)
    ncb = C_Q // LANES
    for seg in range(3):
        nat_ref, d4_ref, d16_ref = out_refs[3 * seg:3 * seg + 3]
        z = jnp.dot(h, w_ref[:, seg * C_Q:(seg + 1) * C_Q], preferred_element_type=F32)
        for cb in range(ncb):
            cs = slice(cb * LANES, (cb + 1) * LANES)
            zc = z[:, cs]
            if seg < 2:
                zc = _rope(zc, tab_ref)
            if seg == 0:
                zc = zc * SCALE
            nat_ref[0, :, cs] = zc.astype(BF16)
            sc_ref[cb] = zc
        for cb in range(ncb):
            cs = slice(cb * LANES, (cb + 1) * LANES)
            for r in range(4):
                d4_ref[0, r, :, cs] = sc_ref[cb, pl.ds(r, tm // 4, stride=4), :].astype(BF16)
            for r in range(16):
                d16_ref[0, r, :, cs] = sc_ref[cb, pl.ds(r, tm // 16, stride=16), :].astype(BF16)


def _proj1(x, g, w, tab, tm=512):
    bsz, seq, d = x.shape
    tile = lambda wd: pl.BlockSpec((1, tm, wd), lambda b, j: (b, j, 0))
    dil = lambda dd: pl.BlockSpec((1, dd, tm // dd, C_Q), lambda b, j: (b, 0, j, 0))
    out_specs, out_shape = [], []
    for _ in range(3):
        out_specs += [tile(C_Q), dil(4), dil(16)]
        out_shape += [jax.ShapeDtypeStruct((bsz, seq, C_Q), BF16),
                      jax.ShapeDtypeStruct((bsz, 4, seq // 4, C_Q), BF16),
                      jax.ShapeDtypeStruct((bsz, 16, seq // 16, C_Q), BF16)]
    return pl.pallas_call(
        _proj1_kernel,
        grid=(bsz, seq // tm),
        in_specs=[tile(d),
                  pl.BlockSpec((1, d), lambda b, j: (0, 0)),
                  pl.BlockSpec(w.shape, lambda b, j: (0, 0)),
                  pl.BlockSpec((tm, 4 * LANES), lambda b, j: (j, 0))],
        out_specs=out_specs,
        out_shape=out_shape,
        scratch_shapes=[pltpu.VMEM((C_Q // LANES, tm, LANES), F32)],
        compiler_params=_cparams(("parallel", "parallel")),
        name="proj1",
    )(x, g.reshape(1, d), w, tab)


def _cmp_mlp_kernel(kc_ref, vc_ref, pe_ref, w1_ref, w2_ref, o_ref):
    nc = o_ref.shape[2]
    for s, src in enumerate((kc_ref, vc_ref)):
        hid_a = jnp.zeros((nc, 2 * CMP_HIDDEN), F32)
        hid_b = jnp.zeros((nc, 2 * CMP_HIDDEN), F32)
        for p in range(CMP_STRIDE):
            xp = src[0, pl.ds(p, nc, stride=CMP_STRIDE), :]
            xa = (xp + pe_ref[s, p:p + 1, :]).astype(BF16)
            xb = (xp + pe_ref[s, CMP_STRIDE + p:CMP_STRIDE + p + 1, :]).astype(BF16)
            hid_a = hid_a + jnp.dot(xa, w1_ref[s, p], preferred_element_type=F32)
            hid_b = hid_b + jnp.dot(xb, w1_ref[s, CMP_STRIDE + p], preferred_element_type=F32)
        hid = hid_a + pltpu.roll(hid_b, nc - 1, 0)
        a = jax.nn.silu(hid).astype(BF16)
        for hh in range(A_KV_HEADS):
            o_ref[0, 2 * s + hh] = jnp.dot(a[:, hh * CMP_HIDDEN:(hh + 1) * CMP_HIDDEN], w2_ref[s],
                                           preferred_element_type=F32)


def _cmp_mlp(kcn, vcn, pe2, w1bd, w2):
    bsz, seq, _ = kcn.shape
    nc = seq // CMP_STRIDE
    nat = pl.BlockSpec((1, seq, LANES), lambda b: (b, 0, 0))
    return pl.pallas_call(
        _cmp_mlp_kernel,
        grid=(bsz,),
        in_specs=[nat, nat,
                  pl.BlockSpec(pe2.shape, lambda b: (0, 0, 0)),
                  pl.BlockSpec(w1bd.shape, lambda b: (0, 0, 0, 0)),
                  pl.BlockSpec(w2.shape, lambda b: (0, 0, 0))],
        out_specs=pl.BlockSpec((1, 2 * A_KV_HEADS, nc, HEAD_DIM), lambda b: (b, 0, 0, 0)),
        out_shape=jax.ShapeDtypeStruct((bsz, 2 * A_KV_HEADS, nc, HEAD_DIM), F32),
        compiler_params=_cparams(("parallel",)),
        name="cmp_mlp",
    )(kcn, vcn, pe2, w1bd, w2)


def _nsa_kernel(qp_ref, qr_ref, gate_ref, kcvc_ref, c2s_ref, ksa_ref, vsa_ref,
                kwd_ref, vwa_ref, o_ref, *, tq, tk):
    i = pl.program_id(1)
    q0 = i * tq
    ncp = kcvc_ref.shape[2]
    nsel = ksa_ref.shape[1] // SEL_BLOCK
    G = A_GROUP
    NH = A_KV_HEADS
    rows = G * tq
    lo = _low_half(tq)
    streams = range(NH)

    def masked_heads(ref, kvh):
        out = []
        for pr in range(G // 2):
            c0 = (kvh * (G // 2) + pr) * LANES
            out.extend(_masked_pairs(ref[0, :, c0:c0 + LANES], lo))
        return out

    tpos_c = q0 + lax.broadcasted_iota(jnp.int32, (tq, ncp), 0)
    cidx = lax.broadcasted_iota(jnp.int32, (tq, ncp), 1)
    cvalid = (cidx * CMP_STRIDE + (CMP_BLOCK - 1) <= tpos_c) & (cidx < ncp - 1)
    cbias = jnp.where(cvalid, 0.0, -jnp.inf).astype(F32)
    cbias4 = jnp.concatenate([cbias] * G, axis=0)
    o_cmp, psum = [], []
    for kvh in streams:
        kc = kcvc_ref[0, kvh].astype(BF16)
        vc = kcvc_ref[0, NH + kvh].astype(BF16)
        kc2 = jnp.concatenate([kc, kc], axis=1)
        vc2 = jnp.concatenate([vc, vc], axis=1)
        qc = jnp.concatenate(masked_heads(qp_ref, kvh), axis=0)
        s = lax.dot_general(qc, kc2, _NT, preferred_element_type=F32) + cbias4
        m = jnp.max(s, axis=-1, keepdims=True)
        m = jnp.where(m == -jnp.inf, 0.0, m)
        p = jnp.exp(s - m)
        den = jnp.sum(p, axis=-1, keepdims=True)
        p = p / jnp.where(den > 0, den, 1.0)
        o_cmp.append(jnp.dot(p.astype(BF16), vc2, preferred_element_type=F32))
        ps = p[0:tq]
        for g in range(1, G):
            ps = ps + p[g * tq:(g + 1) * tq]
        psum.append(ps)

    nkw = NSA_WINDOW + tq
    start = pl.multiple_of(jnp.maximum(q0 - NSA_WINDOW, 0), tq)
    rq_w = q0 + lax.broadcasted_iota(jnp.int32, (tq, nkw), 0)
    kpos = start + lax.broadcasted_iota(jnp.int32, (tq, nkw), 1)
    diff = rq_w - kpos
    wbias = jnp.where((diff >= 0) & (diff < NSA_WINDOW), 0.0, -jnp.inf).astype(F32)
    wbias4 = jnp.concatenate([wbias] * G, axis=0)
    q_heads = [masked_heads(qr_ref, kvh) for kvh in streams]
    acc_win = []
    for kvh in streams:
        kw = kwd_ref[0, pl.ds(start, nkw), kvh * LANES:(kvh + 1) * LANES]
        vw = vwa_ref[0, pl.ds(start, nkw), kvh * LANES:(kvh + 1) * LANES]
        qs = jnp.concatenate(q_heads[kvh], axis=0)
        s = lax.dot_general(qs, kw, _NT, preferred_element_type=F32) + wbias4
        m = jnp.max(s, axis=-1, keepdims=True)
        p = jnp.exp2(s - m)
        acc_win.append(jnp.dot(p.astype(BF16), vw, preferred_element_type=F32))

    c2s = c2s_ref[...]
    blk = lax.broadcasted_iota(jnp.int32, (LANES, tq), 0)
    cur = jnp.right_shift(q0 + lax.broadcasted_iota(jnp.int32, (LANES, tq), 1),
                          SEL_BLOCK.bit_length() - 1)
    forced = (blk == 0) | (blk == cur) | (blk == cur - 1)
    future = blk > cur
    blk_g = lax.broadcasted_iota(jnp.int32, (SUBLANES, tq), 0)
    ngrp = nsel // SUBLANES
    qs_aug = []
    for kvh in streams:
        p_hi = psum[kvh].astype(BF16)
        p_lo = (psum[kvh] - p_hi.astype(F32)).astype(BF16)
        imp = (jnp.dot(p_hi, c2s, preferred_element_type=F32)
               + jnp.dot(p_lo, c2s, preferred_element_type=F32))
        imp_t = jnp.where(forced, jnp.inf, jnp.where(future, -jnp.inf, imp.T))
        imp_g = [imp_t[r * SUBLANES:(r + 1) * SUBLANES] for r in range(ngrp)]
        cnt_g = [jnp.zeros((SUBLANES, tq), jnp.int32) for _ in range(ngrp)]
        for mblk in range(nsel):
            row = imp_t[mblk:mblk + 1, :]
            for r in range(ngrp):
                first = r * SUBLANES
                if first > mblk:
                    inc = (row >= imp_g[r]).astype(jnp.int32)
                elif first + SUBLANES - 1 <= mblk:
                    inc = (row > imp_g[r]).astype(jnp.int32)
                else:
                    inc = jnp.where(blk_g + first > mblk, (row >= imp_g[r]).astype(jnp.int32),
                                    (row > imp_g[r]).astype(jnp.int32))
                cnt_g[r] = cnt_g[r] + inc
        bias_t = jnp.concatenate(
            [jnp.where(c < SEL_TOPK, 0.0, NEG_BIG).astype(F32) for c in cnt_g]
            + [jnp.zeros((LANES - nsel, tq), F32)], axis=0)
        bias = bias_t.T.astype(BF16)
        qs_aug.append(jnp.concatenate(
            [jnp.concatenate([qh, bias], axis=1) for qh in q_heads[kvh]], axis=0))

    def sel_chunk(c, carry, diagonal):
        off = pl.multiple_of(c * tk, tk)
        out = []
        for kvh in streams:
            m, acc = carry[kvh]
            k = ksa_ref[0, pl.ds(off, tk), kvh * 2 * LANES:(kvh + 1) * 2 * LANES]
            v = vsa_ref[0, pl.ds(off, tk), kvh * LANES:(kvh + 1) * LANES]
            s = lax.dot_general(qs_aug[kvh], k, _NT, preferred_element_type=F32)
            if diagonal:
                rq = q0 + (lax.broadcasted_iota(jnp.int32, (rows, tk), 0) & (tq - 1))
                ck = off + lax.broadcasted_iota(jnp.int32, (rows, tk), 1)
                s = jnp.where(ck <= rq, s, NEG_BIG)
            m_new = jnp.maximum(m, jnp.max(s, axis=-1, keepdims=True))
            alpha = jnp.exp2(m - m_new)
            p = jnp.exp2(s - m_new)
            acc = alpha * acc + jnp.dot(p.astype(BF16), v, preferred_element_type=F32)
            out.append((m_new, acc))
        return tuple(out)

    n_full = q0 // tk
    init = tuple((jnp.full((rows, 1), -jnp.inf, F32), jnp.zeros((rows, LANES), F32)) for _ in streams)
    carry = lax.fori_loop(0, n_full, lambda c, cr: sel_chunk(c, cr, False), init)
    carry = sel_chunk(n_full, carry, True)

    gates = gate_ref[0]
    for kvh in streams:
        acc_sel = carry[kvh][1]
        sel_sw = pltpu.roll(acc_sel, HEAD_DIM, 1)
        win_sw = pltpu.roll(acc_win[kvh], HEAD_DIM, 1)
        for pr in range(G // 2):
            halves = []
            for hh in range(2):
                g = 2 * pr + hh
                rs = slice(g * tq, (g + 1) * tq)
                if hh == 0:
                    o_s = acc_sel[rs] / sel_sw[rs]
                    o_w = acc_win[kvh][rs] / win_sw[rs]
                else:
                    o_s = sel_sw[rs] / acc_sel[rs]
                    o_w = win_sw[rs] / acc_win[kvh][rs]
                gc = (kvh * G + g) * N_BRANCH
                halves.append(gates[:, gc:gc + 1] * o_cmp[kvh][rs] + gates[:, gc + 1:gc + 2] * o_s
                              + gates[:, gc + 2:gc + 3] * o_w)
            c0 = (kvh * (G // 2) + pr) * LANES
            o_ref[0, :, c0:c0 + LANES] = jnp.where(lo, halves[0], halves[1]).astype(o_ref.dtype)


def _nsa(qp, qr, gates, kcvc, c2s, ksa, vsa, kwd, vwa, tq=256, tk=1024):
    bsz, seq, _ = qp.shape
    ncp = kcvc.shape[2]
    kern = functools.partial(_nsa_kernel, tq=tq, tk=tk)
    qtile = pl.BlockSpec((1, tq, A_Q), lambda b, i: (b, i, 0))
    res = lambda wd: pl.BlockSpec((1, seq, wd), lambda b, i: (b, 0, 0))
    return pl.pallas_call(
        kern,
        grid=(bsz, seq // tq),
        in_specs=[
            qtile, qtile,
            pl.BlockSpec((1, tq, LANES), lambda b, i: (b, i, 0)),
            pl.BlockSpec((1, 2 * A_KV_HEADS, ncp, HEAD_DIM), lambda b, i: (b, 0, 0, 0)),
            pl.BlockSpec((ncp, LANES), lambda b, i: (0, 0)),
            res(4 * LANES), res(2 * LANES), res(2 * LANES), res(2 * LANES),
        ],
        out_specs=qtile,
        out_shape=jax.ShapeDtypeStruct((bsz, seq, A_Q), BF16),
        compiler_params=_cparams(("parallel", "arbitrary")),
        name="nsa",
    )(qp, qr, gates, kcvc, c2s, ksa, vsa, kwd, vwa)


def _banded_kernel(*refs, npairs, pairs_per_kv, reach, tq, nsub, has_sink, emit_lse):
    q_ref, kp_ref, kc_ref, vp_ref, vc_ref = refs[:5]
    pos = 5
    sink_ref = None
    if has_sink:
        sink_ref = refs[pos]
        pos += 1
    o_ref = refs[pos]
    lse_ref = refs[pos + 1] if emit_lse else None

    i = pl.program_id(2)
    nk = 2 * tq
    lo = _low_half(tq)
    lane = lax.broadcasted_iota(jnp.int32, (tq, LANES), 1)
    row = lax.broadcasted_iota(jnp.int32, (tq, nk), 0)
    col = lax.broadcasted_iota(jnp.int32, (tq, nk), 1)
    diff = tq + row - col
    band = (diff >= 0) & (diff <= reach)
    bias_band = jnp.where(band, 0.0, -jnp.inf).astype(F32)
    bias_first = jnp.where(band & ((col >= tq) | (i > 0)), 0.0, -jnp.inf).astype(F32)
    bias2 = [jnp.concatenate([b, b], axis=0) for b in (bias_first, bias_band)]

    nkv = npairs // pairs_per_kv
    k_all = [jnp.concatenate([kp_ref[0, 0, :, c * LANES:(c + 1) * LANES],
                              kc_ref[0, 0, :, c * LANES:(c + 1) * LANES]], axis=0) for c in range(nkv)]
    v_all = [jnp.concatenate([vp_ref[0, 0, :, c * LANES:(c + 1) * LANES],
                              vc_ref[0, 0, :, c * LANES:(c + 1) * LANES]], axis=0) for c in range(nkv)]

    parts = []
    for j in range(nsub):
        for p in range(npairs):
            qpair = q_ref[0, 0, j * tq:(j + 1) * tq, p * LANES:(p + 1) * LANES]
            q2 = jnp.concatenate(_masked_pairs(qpair, lo), axis=0)
            kk = k_all[p // pairs_per_kv][j * tq:(j + 2) * tq]
            sp = lax.dot_general(q2, kk, _NT, preferred_element_type=F32)
            parts.append(sp + bias2[min(j, 1)])
    s = jnp.concatenate(parts, axis=0)
    m = jnp.max(s, axis=-1, keepdims=True)
    if has_sink:
        sink = jnp.concatenate(
            [jnp.broadcast_to(sink_ref[0:1, 2 * p + hh:2 * p + hh + 1], (tq, 1))
             for j in range(nsub) for p in range(npairs) for hh in range(2)], axis=0)
        m = jnp.maximum(m, sink)
    pexp = jnp.exp(s - m)
    den = jnp.sum(pexp, axis=-1, keepdims=True)
    if has_sink:
        den = den + jnp.exp(sink - m)
    pb = pexp.astype(BF16)
    if emit_lse:
        lse = m + jnp.log(den)
    for j in range(nsub):
        stats = jnp.zeros((tq, LANES), F32)
        for p in range(npairs):
            part = j * npairs + p
            rs = slice(part * 2 * tq, (part + 1) * 2 * tq)
            vv = v_all[p // pairs_per_kv][j * tq:(j + 2) * tq]
            r = jnp.dot(pb[rs], vv, preferred_element_type=F32) / den[rs]
            o_ref[0, 0, j * tq:(j + 1) * tq, p * LANES:(p + 1) * LANES] = (
                jnp.where(lo, r[:tq], r[tq:]).astype(o_ref.dtype))
            if emit_lse:
                for hh in range(2):
                    first = part * 2 * tq + hh * tq
                    stats = jnp.where(lane == 2 * p + hh, lse[first:first + tq], stats)
        if emit_lse:
            lse_ref[0, 0, j * tq:(j + 1) * tq, :] = stats


def _banded(q, k, v, *, pairs_per_kv, reach, nsub, sink=None, emit_lse=False, tq=128):
    bsz, nres, length, cq = q.shape
    ck = k.shape[-1]
    npairs = cq // LANES
    assert reach <= tq and length % (nsub * tq) == 0
    kern = functools.partial(_banded_kernel, npairs=npairs, pairs_per_kv=pairs_per_kv, reach=reach,
                             tq=tq, nsub=nsub, has_sink=sink is not None, emit_lse=emit_lse)
    cur = lambda wd: pl.BlockSpec((1, 1, nsub * tq, wd), lambda b, r, i: (b, r, i, 0))
    prev = lambda wd: pl.BlockSpec((1, 1, tq, wd),
                                   lambda b, r, i: (b, r, jnp.maximum(i * nsub - 1, 0), 0))
    in_specs = [cur(cq), prev(ck), cur(ck), prev(ck), cur(ck)]
    args = [q, k, k, v, v]
    if sink is not None:
        in_specs.append(pl.BlockSpec((1, LANES), lambda b, r, i: (0, 0)))
        args.append(sink)
    out_specs = [cur(cq)]
    out_shape = [jax.ShapeDtypeStruct(q.shape, BF16)]
    if emit_lse:
        out_specs.append(cur(LANES))
        out_shape.append(jax.ShapeDtypeStruct((bsz, nres, length, LANES), F32))
    return pl.pallas_call(
        kern,
        grid=(bsz, nres, length // (nsub * tq)),
        in_specs=in_specs,
        out_specs=out_specs,
        out_shape=out_shape,
        compiler_params=_cparams(("parallel", "parallel", "arbitrary")),
        name="banded",
    )(*args)


def _outproj2_kernel(x_ref, a_ref, b_ref, w_ref, o_ref):
    mix = jnp.concatenate([a_ref[...], b_ref[...]], axis=-1)
    o_ref[...] = x_ref[...] + jnp.dot(mix, w_ref[...], preferred_element_type=F32)


def _outproj2(x, a, b, w, tm=512):
    n, d = x.shape
    row = lambda wd: pl.BlockSpec((tm, wd), lambda i: (i, 0))
    return pl.pallas_call(
        _outproj2_kernel,
        grid=(n // tm,),
        in_specs=[row(d), row(a.shape[1]), row(b.shape[1]),
                  pl.BlockSpec(w.shape, lambda i: (0, 0))],
        out_specs=row(d),
        out_shape=jax.ShapeDtypeStruct((n, d), F32),
        compiler_params=_cparams(("parallel",)),
        name="outproj2",
    )(x, a, b, w)


def _outproj_merge_kernel(x_ref, o1_ref, o4_ref, o16_ref, l1_ref, l4_ref, l16_ref, e_ref, w_ref,
                          o_ref, sl4_ref, sl16_ref, so4_ref, so16_ref):
    tm = x_ref.shape[1]
    for r in range(4):
        sl4_ref[pl.ds(r, tm // 4, stride=4), :] = l4_ref[0, r]
    for r in range(16):
        sl16_ref[pl.ds(r, tm // 16, stride=16), :] = l16_ref[0, r]
    ncb = C_Q // LANES
    for cb in range(ncb):
        cs = slice(cb * LANES, (cb + 1) * LANES)
        for r in range(4):
            so4_ref[cb, pl.ds(r, tm // 4, stride=4), :] = o4_ref[0, r, :, cs].astype(F32)
        for r in range(16):
            so16_ref[cb, pl.ds(r, tm // 16, stride=16), :] = o16_ref[0, r, :, cs].astype(F32)
    lses = [l1_ref[0], sl4_ref[...], sl16_ref[...]]
    m = jnp.maximum(jnp.maximum(lses[0], lses[1]), lses[2])
    es = [jnp.exp(l - m) for l in lses]
    den = es[0] + es[1] + es[2]
    e = e_ref[...]
    w_exp = []
    for ex in es:
        wgt = ex / den
        w_hi = wgt.astype(BF16)
        w_lo = (wgt - w_hi.astype(F32)).astype(BF16)
        w_exp.append(jnp.dot(w_hi, e, preferred_element_type=F32)
                     + jnp.dot(w_lo, e, preferred_element_type=F32))
    mix = []
    for cb in range(ncb):
        cs = slice(cb * LANES, (cb + 1) * LANES)
        mix.append((w_exp[0][:, cs] * o1_ref[0, :, cs].astype(F32)
                    + w_exp[1][:, cs] * so4_ref[cb]
                    + w_exp[2][:, cs] * so16_ref[cb]).astype(BF16))
    o_ref[0] = x_ref[0] + jnp.dot(jnp.concatenate(mix, axis=-1), w_ref[...],
                                  preferred_element_type=F32)


def _outproj_merge(x, outs, lses, expand, w, tm=512):
    bsz, seq, d = x.shape
    tile = lambda wd: pl.BlockSpec((1, tm, wd), lambda b, j: (b, j, 0))
    dil = lambda dd, wd: pl.BlockSpec((1, dd, tm // dd, wd), lambda b, j: (b, 0, j, 0))
    return pl.pallas_call(
        _outproj_merge_kernel,
        grid=(bsz, seq // tm),
        in_specs=[tile(d), tile(C_Q), dil(4, C_Q), dil(16, C_Q),
                  tile(LANES), dil(4, LANES), dil(16, LANES),
                  pl.BlockSpec(expand.shape, lambda b, j: (0, 0)),
                  pl.BlockSpec(w.shape, lambda b, j: (0, 0))],
        out_specs=tile(d),
        out_shape=jax.ShapeDtypeStruct((bsz, seq, d), F32),
        scratch_shapes=[pltpu.VMEM((tm, LANES), F32), pltpu.VMEM((tm, LANES), F32),
                        pltpu.VMEM((C_Q // LANES, tm, LANES), F32),
                        pltpu.VMEM((C_Q // LANES, tm, LANES), F32)],
        compiler_params=_cparams(("parallel", "parallel")),
        name="outproj_merge",
    )(x, *outs, *lses, expand, w)


def _ffn_kernel(x_ref, g_ref, wgu_ref, wd_ref, fg_ref, o_ref, acc_ref, *, nchunk, final):
    x = x_ref[...]
    h = _rms(x, g_ref[...]).astype(BF16)
    acc_ref[...] = jnp.zeros_like(acc_ref)

    def body(c, carry):
        gate = jnp.dot(h, wgu_ref[c], preferred_element_type=F32)
        up = jnp.dot(h, wgu_ref[nchunk + c], preferred_element_type=F32)
        a = (jax.nn.silu(gate) * up).astype(BF16)
        acc_ref[...] += jnp.dot(a, wd_ref[c], preferred_element_type=F32)
        return carry

    lax.fori_loop(0, nchunk, body, 0)
    y = x + acc_ref[...]
    if final:
        y = _rms(y, fg_ref[...])
    o_ref[...] = y


def _ffn(x, g, wgu3, wd3, fg, final, tm=512):
    n, d = x.shape
    nchunk = wd3.shape[0]
    kern = functools.partial(_ffn_kernel, nchunk=nchunk, final=final)
    return pl.pallas_call(
        kern,
        grid=(n // tm,),
        in_specs=[
            pl.BlockSpec((tm, d), lambda i: (i, 0)),
            pl.BlockSpec((1, d), lambda i: (0, 0)),
            pl.BlockSpec(wgu3.shape, lambda i: (0, 0, 0)),
            pl.BlockSpec(wd3.shape, lambda i: (0, 0, 0)),
            pl.BlockSpec((1, d), lambda i: (0, 0)),
        ],
        out_specs=pl.BlockSpec((tm, d), lambda i: (i, 0)),
        out_shape=jax.ShapeDtypeStruct((n, d), F32),
        scratch_shapes=[pltpu.VMEM((tm, d), F32)],
        compiler_params=_cparams(("parallel",)),
        name="ffn",
    )(x, g.reshape(1, d), wgu3, wd3, fg.reshape(1, d))


def _tables(seq):
    inv = 1.0 / (ROPE_THETA ** (jnp.arange(0, HEAD_DIM, 2, dtype=F32) / HEAD_DIM))
    ang = jnp.arange(seq, dtype=F32)[:, None] * inv[None, :]
    ang = jnp.concatenate([ang, ang], axis=-1)
    cos, sin = jnp.cos(ang), jnp.sin(ang)
    first = (jnp.arange(HEAD_DIM) < HEAD_DIM // 2)[None, :]
    sin_lo = jnp.where(first, -sin, 0.0)
    sin_hi = jnp.where(first, 0.0, sin)
    tile2 = lambda t: jnp.concatenate([t, t], axis=-1)
    onehot = (jnp.arange(seq)[:, None] // SEL_BLOCK == jnp.arange(LANES)[None, :]).astype(F32)
    return jnp.concatenate([tile2(cos), tile2(sin_lo), tile2(sin_hi), onehot], axis=-1).astype(F32)


def _cmp_to_sel(ncp, nsel):
    c_start = np.arange(ncp) * CMP_STRIDE
    s_start = np.arange(LANES) * SEL_BLOCK
    overlap = ((c_start[:, None] <= s_start[None, :] + SEL_BLOCK - 1)
               & (c_start[:, None] + CMP_BLOCK - 1 >= s_start[None, :]))
    overlap &= (np.arange(ncp) < ncp - 1)[:, None] & (np.arange(LANES) < nsel)[None, :]
    return jnp.asarray(overlap.astype(np.float32), dtype=BF16)


def _cmp_weights(pe_k, pe_v, k_w1, v_w1, k_w2, v_w2):
    pe2 = jnp.stack([jnp.concatenate([pe, pe], axis=-1) for pe in (pe_k, pe_v)])
    w1 = jnp.stack([k_w1, v_w1]).astype(BF16).reshape(2, CMP_BLOCK, HEAD_DIM, CMP_HIDDEN)
    zero = jnp.zeros_like(w1)
    w1bd = jnp.concatenate([jnp.concatenate([w1, zero], axis=-1),
                            jnp.concatenate([zero, w1], axis=-1)], axis=-2)
    w2 = jnp.stack([k_w2, v_w2]).astype(BF16)
    return pe2, w1bd, w2


def _ffn_weights(w_gate_up, w_down, th=1408):
    d = w_gate_up.shape[0]
    nchunk = FFN_HIDDEN // th
    wgu3 = w_gate_up.astype(BF16).reshape(d, 2 * nchunk, th).transpose(1, 0, 2)
    wd3 = w_down.astype(BF16).reshape(nchunk, th, d)
    return wgu3, wd3


def kernel(x, attn_norm, ffn_norm, final_norm, w_in_e, w_out_e, cmp_pe_k, cmp_pe_v,
           cmp_k_w1, cmp_k_w2, cmp_v_w1, cmp_v_w2, sinks, w_qkv_o, w_out_o,
           w_gate_up, w_down):
    bsz, seq, d = x.shape
    n = bsz * seq
    tab = _tables(seq)

    w = w_in_e[0]
    o1 = A_Q
    o2 = o1 + 6 * A_KV
    o3 = o2 + A_GATES
    o4 = o3 + B_Q
    kv = lambda j: w[:, o1 + j * A_KV:o1 + (j + 1) * A_KV]
    gate_w = jnp.pad(w[:, o2:o3], ((0, 0), (0, LANES - A_GATES)))
    w0 = jnp.concatenate(
        [w[:, :o1], kv(2), kv(4), w[:, o3:o4], w[:, o4:o4 + B_KV],
         kv(0), kv(1), kv(3), kv(5), w[:, o4 + B_KV:], gate_w], axis=1).astype(BF16)
    (qa_p, qa_r, ksa, kwd, qb_r, kbd, kcn, vcn, vsa, vwa, vbd, gates) = _proj0(
        x, attn_norm[0], w0, tab)

    pe2, w1bd, w2 = _cmp_weights(cmp_pe_k[0], cmp_pe_v[0], cmp_k_w1[0], cmp_v_w1[0],
                                 cmp_k_w2[0], cmp_v_w2[0])
    kcvc = _cmp_mlp(kcn, vcn, pe2, w1bd, w2)

    a_out = _nsa(qa_p, qa_r, gates, kcvc, _cmp_to_sel(seq // CMP_STRIDE, seq // SEL_BLOCK),
                 ksa, vsa, kwd, vwa)

    sink_row = jnp.pad(sinks[0].astype(F32), (0, LANES - B_HEADS)).reshape(1, LANES)
    r4 = lambda t: t.reshape(bsz, 1, seq, t.shape[-1])
    (b_out,) = _banded(r4(qb_r), r4(kbd), r4(vbd), pairs_per_kv=B_GROUP // 2,
                       reach=B_WINDOW - 1, nsub=4, sink=sink_row)

    xf = _outproj2(x.reshape(n, d), a_out.reshape(n, A_Q), b_out.reshape(n, B_Q),
                   w_out_e[0].astype(BF16))
    wgu3, wd3 = _ffn_weights(w_gate_up[0], w_down[0])
    xf = _ffn(xf, ffn_norm[0], wgu3, wd3, final_norm, final=False)

    x1 = xf.reshape(bsz, seq, d)
    (q_n, q_4, q_16, k_n, k_4, k_16, v_n, v_4, v_16) = _proj1(
        x1, attn_norm[1], w_qkv_o[0].astype(BF16), tab)
    pat_o, pat_l = [], []
    for (window, dil), (qq, kk, vv) in zip(
            C_PATTERNS, ((r4(q_n), r4(k_n), r4(v_n)), (q_4, k_4, v_4), (q_16, k_16, v_16))):
        o_p, l_p = _banded(qq, kk, vv, pairs_per_kv=1, reach=window // dil,
                           nsub=2, emit_lse=True)
        pat_o.append(o_p)
        pat_l.append(l_p)
    pat_o[0] = pat_o[0].reshape(bsz, seq, C_Q)
    pat_l[0] = pat_l[0].reshape(bsz, seq, LANES)
    expand = (jnp.arange(LANES)[:, None] == jnp.arange(C_Q)[None, :] // HEAD_DIM).astype(BF16)
    x2 = _outproj_merge(x1, pat_o, pat_l, expand, w_out_o[0].astype(BF16))
    wgu3, wd3 = _ffn_weights(w_gate_up[1], w_down[1])
    xf = _ffn(x2.reshape(n, d), ffn_norm[1], wgu3, wd3, final_norm, final=True)
    return xf.reshape(bsz, seq, d)
```
